```python
import math
import jax, jax.numpy as jnp
from jax import lax
import numpy as np

D_MODEL = 1024
BATCH = 2
SEQ = 16384
DEPTH = 4
DEC_BATCH = 16
DEC_SEQ = 32
PAST_LEN = 1024

CHUNK = 64
N_META = 16
N_MIXERS = 2
N_SSD_LAYERS = (DEPTH + 1) // 2
N_GLA_LAYERS = DEPTH // 2

SSD_INNER = 2 * D_MODEL
SSD_HEAD_DIM = 64
SSD_HEADS = SSD_INNER // SSD_HEAD_DIM
SSD_GROUPS = 4
SSD_HPG = SSD_HEADS // SSD_GROUPS
SSD_STATE = 128
SSD_CONV_W = 4
SSD_CONV_DIM = SSD_INNER + 2 * SSD_GROUPS * SSD_STATE
SSD_IN_DIM = SSD_INNER + SSD_CONV_DIM + SSD_HEADS

GLA_HEADS = 4
GLA_KEY_DIM = D_MODEL // 2
GLA_VAL_DIM = D_MODEL
GLA_DK = GLA_KEY_DIM // GLA_HEADS
GLA_DV = GLA_VAL_DIM // GLA_HEADS
GLA_GATE_RANK = 16
GLA_GATE_NORM = 16.0
GLA_IN_DIM = 2 * GLA_KEY_DIM + 2 * GLA_VAL_DIM + GLA_GATE_RANK

MOE_GROUPS = 4
MOE_EXP_PER_GROUP = 8
MOE_EXPERTS = MOE_GROUPS * MOE_EXP_PER_GROUP
MOE_TOPK = 2
EXPERT_FF = 512
MOE_BLOCK = 128

DEEPNORM_ALPHA = (2.0 * DEPTH) ** 0.25
DEEPNORM_BETA = (8.0 * DEPTH) ** -0.25
NORM_EPS = 1e-5

kernel_name = 'hybrid_ssd_gla_hmoe_stream_step'


def layer_norm(x, g, b):
    xf = x.astype(jnp.float32)
    mu = jnp.mean(xf, -1, keepdims=True)
    var = jnp.mean(jnp.square(xf - mu), -1, keepdims=True)
    return ((xf - mu) * lax.rsqrt(var + NORM_EPS) * g.astype(jnp.float32) + b.astype(jnp.float32)).astype(x.dtype)


def rms_norm(x, w):
    xf = x.astype(jnp.float32)
    return (xf * lax.rsqrt(jnp.mean(xf * xf, -1, keepdims=True) + NORM_EPS) * w.astype(jnp.float32)).astype(x.dtype)


def causal_dwconv(x, buf, w, b):
    xc = jnp.concatenate([buf.astype(x.dtype), x], axis=1)
    y = lax.conv_general_dilated(xc, w[:, None, :].astype(x.dtype), window_strides=(1,), padding='VALID',
                                 dimension_numbers=('NWC', 'WIO', 'NWC'), feature_group_count=x.shape[-1])
    return y + b.astype(x.dtype), xc[:, -(SSD_CONV_W - 1):]


def _chunked(arrs, length):
    pad = (-length) % CHUNK
    out = []
    for a in arrs:
        a = jnp.pad(a, [(0, 0), (pad, 0)] + [(0, 0)] * (a.ndim - 2))
        b, lp = a.shape[:2]
        out.append(jnp.moveaxis(a.reshape(b, lp // CHUNK, CHUNK, *a.shape[2:]), 1, 0))
    return out, pad


def _unchunk(y, pad):
    nc, b = y.shape[:2]
    y = jnp.moveaxis(y, 0, 1).reshape(b, nc * CHUNK, *y.shape[3:])
    return y[:, pad:]


def ssd_scan(xs, dt, a_neg, Bm, Cm, S0):
    f32 = jnp.float32
    b, L = xs.shape[:2]
    xdt = (xs.astype(f32) * dt[..., None]).reshape(b, L, SSD_GROUPS, SSD_HPG, SSD_HEAD_DIM)
    loga = (dt * a_neg).reshape(b, L, SSD_GROUPS, SSD_HPG)
    (x_c, a_c, B_c, C_c), pad = _chunked([xdt, loga, Bm.astype(f32), Cm.astype(f32)], L)
    causal = jnp.tril(jnp.ones((CHUNK, CHUNK), bool))[None, :, :, None, None]

    def step(S, inp):
        xc, ac, Bc, Cc = inp
        cum = jnp.cumsum(ac, axis=1)
        decay = jnp.exp(jnp.where(causal, cum[:, :, None] - cum[:, None], -jnp.inf))
        W = jnp.einsum('btgn,bsgn->btsg', Cc, Bc)[..., None] * decay
        y = (jnp.einsum('btsgr,bsgrp->btgrp', W, xc)
             + jnp.einsum('btgn,bgrpn->btgrp', Cc, S) * jnp.exp(cum)[..., None])
        to_end = jnp.exp(cum[:, -1:] - cum)[..., None]
        S = (jnp.exp(cum[:, -1])[..., None, None] * S
             + jnp.einsum('bsgn,bsgrp->bgrpn', Bc, xc * to_end))
        return S, y

    S0 = S0.astype(f32).reshape(b, SSD_GROUPS, SSD_HPG, SSD_HEAD_DIM, SSD_STATE)
    S, y = lax.scan(step, S0, (x_c, a_c, B_c, C_c))
    y = _unchunk(y, pad).reshape(b, L, SSD_HEADS, SSD_HEAD_DIM)
    return y, S.reshape(b, SSD_HEADS, SSD_HEAD_DIM, SSD_STATE)


def ssd_mixer(x, S0, conv_buf, w_in, conv_w, conv_b, dt_bias, A_log, D_skip, norm_w, w_out):
    b, L, _ = x.shape
    f32 = jnp.float32
    proj = x @ w_in
    z, xBC, dt_raw = jnp.split(proj, [SSD_INNER, SSD_INNER + SSD_CONV_DIM], axis=-1)
    xBC, new_buf = causal_dwconv(xBC, conv_buf, conv_w, conv_b)
    xBC = jax.nn.silu(xBC)
    xs, Bm, Cm = jnp.split(xBC, [SSD_INNER, SSD_INNER + SSD_GROUPS * SSD_STATE], axis=-1)
    xs = xs.reshape(b, L, SSD_HEADS, SSD_HEAD_DIM)
    Bm = Bm.reshape(b, L, SSD_GROUPS, SSD_STATE)
    Cm = Cm.reshape(b, L, SSD_GROUPS, SSD_STATE)
    dt = jax.nn.softplus(dt_raw.astype(f32) + dt_bias.astype(f32))
    a_neg = -jnp.exp(A_log.astype(f32))
    y, S = ssd_scan(xs, dt, a_neg, Bm, Cm, S0)
    y = y + D_skip.astype(f32)[:, None] * xs.astype(f32)
    y = y.reshape(b, L, SSD_INNER) * jax.nn.silu(z.astype(f32))
    y = rms_norm(y.reshape(b, L, SSD_GROUPS, SSD_INNER // SSD_GROUPS), jnp.ones((), f32))
    y = (y.reshape(b, L, SSD_INNER) * norm_w.astype(f32)).astype(x.dtype)
    return y @ w_out, S.astype(x.dtype), new_buf


def gla_scan(q, k, v, logg, S0):
    f32 = jnp.float32
    b, L = q.shape[:2]
    (q_c, k_c, v_c, g_c), pad = _chunked([q.astype(f32), k.astype(f32), v.astype(f32), logg], L)
    causal = jnp.tril(jnp.ones((CHUNK, CHUNK), bool))[None, :, :, None, None]

    def step(S, inp):
        qc, kc, vc, gc = inp
        cum = jnp.cumsum(gc, axis=1)
        decay = jnp.exp(jnp.where(causal, cum[:, :, None] - cum[:, None], -jnp.inf))
        A = jnp.einsum('bthk,bshk,btshk->bhts', qc, kc, decay)
        y = (jnp.einsum('bhts,bshv->bthv', A, vc)
             + jnp.einsum('bthk,bhkv->bthv', qc * jnp.exp(cum), S))
        to_end = jnp.exp(cum[:, -1:] - cum)
        S = (jnp.exp(cum[:, -1])[..., None] * S
             + jnp.einsum('bshk,bshv->bhkv', kc * to_end, vc))
        return S, y

    S, y = lax.scan(step, S0.astype(f32), (q_c, k_c, v_c, g_c))
    return _unchunk(y, pad), S


def gla_mixer(x, S0, w_in, w_gk2, b_gk2, norm_w, w_out):
    b, L, _ = x.shape
    proj = x @ w_in
    q, k, v, g, gk_lr = jnp.split(
        proj, [GLA_KEY_DIM, 2 * GLA_KEY_DIM, 2 * GLA_KEY_DIM + GLA_VAL_DIM, 2 * GLA_KEY_DIM + 2 * GLA_VAL_DIM], axis=-1)
    logg = jax.nn.log_sigmoid((gk_lr @ w_gk2 + b_gk2).astype(jnp.float32)) / GLA_GATE_NORM
    q = (q * GLA_DK ** -0.5).reshape(b, L, GLA_HEADS, GLA_DK)
    k = k.reshape(b, L, GLA_HEADS, GLA_DK)
    v = v.reshape(b, L, GLA_HEADS, GLA_DV)
    logg = logg.reshape(b, L, GLA_HEADS, GLA_DK)
    o, S = gla_scan(q, k, v, logg, S0)
    o = rms_norm(o, norm_w).reshape(b, L, GLA_VAL_DIM)
    o = (o * jax.nn.silu(g.astype(jnp.float32))).astype(x.dtype)
    return o @ w_out, S.astype(x.dtype)


def routed_experts(xt, e_idx, w_tok, w_gate, w_up, w_down):
    T, D = xt.shape
    M = T * MOE_TOPK
    n_blocks = (M + MOE_EXPERTS * (MOE_BLOCK - 1) + MOE_BLOCK - 1) // MOE_BLOCK
    flat_e = e_idx.reshape(-1).astype(jnp.int32)
    flat_tok = jnp.repeat(jnp.arange(T, dtype=jnp.int32), MOE_TOPK)
    flat_w = w_tok.reshape(-1)
    order = jnp.argsort(flat_e)
    se = flat_e[order]
    counts = jax.ops.segment_sum(jnp.ones_like(flat_e), flat_e, num_segments=MOE_EXPERTS)
    starts = jnp.cumsum(counts) - counts
    pcounts = (counts + MOE_BLOCK - 1) // MOE_BLOCK * MOE_BLOCK
    pends = jnp.cumsum(pcounts)
    pstarts = pends - pcounts
    dest = pstarts[se] + (jnp.arange(M, dtype=jnp.int32) - starts[se])
    row_tok = jnp.full((n_blocks * MOE_BLOCK,), T, jnp.int32).at[dest].set(flat_tok[order])
    row_w = jnp.zeros((n_blocks * MOE_BLOCK,), flat_w.dtype).at[dest].set(flat_w[order])
    block_e = jnp.minimum(jnp.searchsorted(pends, jnp.arange(n_blocks, dtype=jnp.int32) * MOE_BLOCK, side='right'),
                          MOE_EXPERTS - 1)
    xpad = jnp.concatenate([xt, jnp.zeros((1, D), xt.dtype)], axis=0)
    xb = xpad[row_tok].reshape(n_blocks, MOE_BLOCK, D)

    def expert_block(args):
        xblk, e = args
        h = jax.nn.silu(xblk @ w_gate[e]) * (xblk @ w_up[e])
        return h @ w_down[e]

    yb = lax.map(expert_block, (xb, block_e))
    y = jnp.zeros((T + 1, D), jnp.float32).at[row_tok].add(
        yb.reshape(-1, D).astype(jnp.float32) * row_w[:, None])
    return y[:T].astype(xt.dtype)


def hier_moe(x, w_grp, b_grp, w_exp, b_exp, w_gate, w_up, w_down):
    b, L, D = x.shape
    xt = x.reshape(-1, D)
    T = xt.shape[0]
    grp_logits = (xt @ w_grp).astype(jnp.float32) + b_grp.astype(jnp.float32)
    grp_prob = jax.nn.softmax(grp_logits, axis=-1)
    _, g_sel = lax.top_k(grp_logits, 1)
    p_g = jnp.take_along_axis(grp_prob, g_sel, axis=1)
    exp_logits = ((xt @ w_exp).astype(jnp.float32) + b_exp.astype(jnp.float32)).reshape(
        T, MOE_GROUPS, MOE_EXP_PER_GROUP)
    in_grp = exp_logits[jnp.arange(T), g_sel[:, 0]]
    top_v, top_i = lax.top_k(in_grp, MOE_TOPK)
    w_tok = jax.nn.softmax(top_v, axis=-1) * p_g
    e_idx = g_sel * MOE_EXP_PER_GROUP + top_i
    return routed_experts(xt, e_idx, w_tok, w_gate, w_up, w_down).reshape(b, L, D)


def trunk(h, ssd_state, ssd_conv, gla_state, p):
    new_ssd, new_conv, new_gla = [], [], []
    for i in range(DEPTH):
        j = i // N_MIXERS
        if i % N_MIXERS == 0:
            mix, s, buf = ssd_mixer(h, ssd_state[j], ssd_conv[j], p['ssd_w_in'][j], p['ssd_conv_w'][j],
                                    p['ssd_conv_b'][j], p['ssd_dt_bias'][j], p['ssd_A_log'][j], p['ssd_D'][j],
                                    p['ssd_norm_w'][j], p['ssd_w_out'][j])
            new_ssd.append(s)
            new_conv.append(buf)
        else:
            mix, s = gla_mixer(h, gla_state[j], p['gla_w_in'][j], p['gla_w_gk2'][j], p['gla_b_gk2'][j],
                               p['gla_norm_w'][j], p['gla_w_out'][j])
            new_gla.append(s)
        h = layer_norm(DEEPNORM_ALPHA * h + mix, p['ln1_g'][i], p['ln1_b'][i])
        ffn = hier_moe(h, p['moe_w_grp'][i], p['moe_b_grp'][i], p['moe_w_exp'][i], p['moe_b_exp'][i],
                       p['moe_w_gate'][i], p['moe_w_up'][i], p['moe_w_down'][i])
        h = layer_norm(DEEPNORM_ALPHA * h + ffn, p['ln2_g'][i], p['ln2_b'][i])
    return h, jnp.stack(new_ssd), jnp.stack(new_conv), jnp.stack(new_gla)


def setup_inputs(seed: int = 0) -> dict:
    key = jax.random.key(seed)
    ks = iter(jax.random.split(key, 40))
    f32 = jnp.float32

    def nrm(shape, scale):
        return scale * jax.random.normal(next(ks), shape, f32)

    NS, NG = N_SSD_LAYERS, N_GLA_LAYERS
    u = jax.random.uniform(next(ks), (NS, SSD_HEADS), f32)
    dt0 = jnp.exp(u * (math.log(0.1) - math.log(1e-3)) + math.log(1e-3))
    return {
        'x_prompt': nrm((BATCH, SEQ, D_MODEL), 1.0),
        'x_sample': nrm((DEC_BATCH, DEC_SEQ, D_MODEL), 1.0),
        'state_ssd': nrm((NS, DEC_BATCH, SSD_HEADS, SSD_HEAD_DIM, SSD_STATE), 0.1),
        'state_ssd_conv': nrm((NS, DEC_BATCH, SSD_CONV_W - 1, SSD_CONV_DIM), 1.0),
        'state_gla': nrm((NG, DEC_BATCH, GLA_HEADS, GLA_DK, GLA_DV), 0.5),
        'meta_tokens': nrm((N_META, D_MODEL), 1.0),
        'ssd_w_in': nrm((NS, D_MODEL, SSD_IN_DIM), D_MODEL ** -0.5),
        'ssd_conv_w': nrm((NS, SSD_CONV_W, SSD_CONV_DIM), SSD_CONV_W ** -0.5),
        'ssd_conv_b': nrm((NS, SSD_CONV_DIM), 0.02),
        'ssd_dt_bias': dt0 + jnp.log(-jnp.expm1(-dt0)),
        'ssd_A_log': jnp.log(jax.random.uniform(next(ks), (NS, SSD_HEADS), f32, minval=1.0, maxval=16.0)),
        'ssd_D': 1.0 + nrm((NS, SSD_HEADS), 0.1),
        'ssd_norm_w': 1.0 + nrm((NS, SSD_INNER), 0.02),
        'ssd_w_out': nrm((NS, SSD_INNER, D_MODEL), DEEPNORM_BETA * SSD_INNER ** -0.5),
        'gla_w_in': nrm((NG, D_MODEL, GLA_IN_DIM), D_MODEL ** -0.5),
        'gla_w_gk2': nrm((NG, GLA_GATE_RANK, GLA_KEY_DIM), GLA_GATE_RANK ** -0.5),
        'gla_b_gk2': nrm((NG, GLA_KEY_DIM), 0.1),
        'gla_norm_w': 1.0 + nrm((NG, GLA_DV), 0.02),
        'gla_w_out': nrm((NG, GLA_VAL_DIM, D_MODEL), DEEPNORM_BETA * GLA_VAL_DIM ** -0.5),
        'ln1_g': 1.0 + nrm((DEPTH, D_MODEL), 0.02),
        'ln1_b': nrm((DEPTH, D_MODEL), 0.02),
        'moe_w_grp': nrm((DEPTH, D_MODEL, MOE_GROUPS), D_MODEL ** -0.5),
        'moe_b_grp': nrm((DEPTH, MOE_GROUPS), 0.01),
        'moe_w_exp': nrm((DEPTH, D_MODEL, MOE_EXPERTS), D_MODEL ** -0.5),
        'moe_b_exp': nrm((DEPTH, MOE_EXPERTS), 0.01),
        'moe_w_gate': nrm((DEPTH, MOE_EXPERTS, D_MODEL, EXPERT_FF), D_MODEL ** -0.5),
        'moe_w_up': nrm((DEPTH, MOE_EXPERTS, D_MODEL, EXPERT_FF), D_MODEL ** -0.5),
        'moe_w_down': nrm((DEPTH, MOE_EXPERTS, EXPERT_FF, D_MODEL), DEEPNORM_BETA * EXPERT_FF ** -0.5),
        'ln2_g': 1.0 + nrm((DEPTH, D_MODEL), 0.02),
        'ln2_b': nrm((DEPTH, D_MODEL), 0.02),
    }


def reference(x_prompt, x_sample, state_ssd, state_ssd_conv, state_gla, meta_tokens,
              ssd_w_in, ssd_conv_w, ssd_conv_b, ssd_dt_bias, ssd_A_log, ssd_D, ssd_norm_w, ssd_w_out,
              gla_w_in, gla_w_gk2, gla_b_gk2, gla_norm_w, gla_w_out,
              ln1_g, ln1_b, moe_w_grp, moe_b_grp, moe_w_exp, moe_b_exp, moe_w_gate, moe_w_up, moe_w_down,
              ln2_g, ln2_b):
    p = dict(ssd_w_in=ssd_w_in, ssd_conv_w=ssd_conv_w, ssd_conv_b=ssd_conv_b, ssd_dt_bias=ssd_dt_bias,
             ssd_A_log=ssd_A_log, ssd_D=ssd_D, ssd_norm_w=ssd_norm_w, ssd_w_out=ssd_w_out,
             gla_w_in=gla_w_in, gla_w_gk2=gla_w_gk2, gla_b_gk2=gla_b_gk2, gla_norm_w=gla_norm_w,
             gla_w_out=gla_w_out, ln1_g=ln1_g, ln1_b=ln1_b, moe_w_grp=moe_w_grp, moe_b_grp=moe_b_grp,
             moe_w_exp=moe_w_exp, moe_b_exp=moe_b_exp, moe_w_gate=moe_w_gate, moe_w_up=moe_w_up,
             moe_w_down=moe_w_down, ln2_g=ln2_g, ln2_b=ln2_b)
    bp = x_prompt.shape[0]
    dtp = x_prompt.dtype
    meta = jnp.broadcast_to(meta_tokens.astype(dtp)[None], (bp, N_META, D_MODEL))
    h0 = jnp.concatenate([meta, x_prompt], axis=1)
    zs = jnp.zeros((N_SSD_LAYERS, bp, SSD_HEADS, SSD_HEAD_DIM, SSD_STATE), dtp)
    zc = jnp.zeros((N_SSD_LAYERS, bp, SSD_CONV_W - 1, SSD_CONV_DIM), dtp)
    zg = jnp.zeros((N_GLA_LAYERS, bp, GLA_HEADS, GLA_DK, GLA_DV), dtp)
    hp, ssd_p, conv_p, gla_p = trunk(h0, zs, zc, zg, p)
    y_prompt = hp[:, N_META:]
    y_sample, ssd_s, conv_s, gla_s = trunk(x_sample, state_ssd, state_ssd_conv, state_gla, p)
    return (y_prompt, y_sample, ssd_p, conv_p, gla_p, ssd_s, conv_s, gla_s)
```

```python
import functools
import math

import jax
import jax.numpy as jnp
from jax import lax
from jax.experimental import pallas as pl
from jax.experimental.pallas import tpu as pltpu

F32 = jnp.float32
BF16 = jnp.bfloat16
I32 = jnp.int32
HI = lax.Precision.HIGHEST
NT = (((1,), (1,)), ((), ()))
TN = (((0,), (0,)), ((), ()))

D_MODEL = 1024
N_META = 16
SSD_INNER = 2048
SSD_HEAD_DIM = 64
SSD_HEADS = 32
SSD_GROUPS = 4
SSD_STATE = 128
SSD_CONV_W = 4
SSD_CONV_DIM = 3072
GLA_HEADS = 4
GLA_KEY_DIM = 512
GLA_VAL_DIM = 1024
GLA_DK = 128
GLA_DV = 256
GLA_GATE_RANK = 16
GLA_GATE_NORM = 16.0
MOE_GROUPS = 4
MOE_EXP_PER_GROUP = 8
MOE_EXPERTS = 32
EXPERT_FF = 512
NORM_EPS = 1e-5

LANES = 128
VMEM_LIMIT_MB = 56
ROW_TILE = 256
PROMPT_CHUNK = 128
GLA_SUB = 16
EXPERT_BLOCK = 256


def _params(sem):
    return pltpu.CompilerParams(dimension_semantics=sem, vmem_limit_bytes=VMEM_LIMIT_MB << 20)


def _sigmoid(x):
    return 1.0 / (1.0 + jnp.exp(-x))


def _softplus(x):
    return jnp.maximum(x, 0.0) + jnp.log1p(jnp.exp(-jnp.abs(x)))


def _iota(shape, dim):
    return lax.broadcasted_iota(I32, shape, dim)


def _proj_kernel(x_ref, *refs):
    n = len(refs) // 2
    xb = x_ref[...].astype(BF16)
    for w_ref, o_ref in zip(refs[:n], refs[n:]):
        o_ref[...] = jnp.dot(xb, w_ref[...], preferred_element_type=F32).astype(o_ref.dtype)


def _proj(x, ws, out_dtypes):
    rows, k = x.shape
    grid = (rows // ROW_TILE,)
    in_specs = [pl.BlockSpec((ROW_TILE, k), lambda i: (i, 0))]
    in_specs += [pl.BlockSpec(w.shape, lambda i: (0, 0)) for w in ws]
    out_specs = [pl.BlockSpec((ROW_TILE, w.shape[1]), lambda i: (i, 0)) for w in ws]
    out_shape = [jax.ShapeDtypeStruct((rows, w.shape[1]), dt) for w, dt in zip(ws, out_dtypes)]
    return pl.pallas_call(
        _proj_kernel, grid=grid, in_specs=in_specs, out_specs=out_specs, out_shape=out_shape,
        compiler_params=_params(("parallel",)))(x, *ws)


def _ssd_kernel(z_ref, xbc_ref, dt_ref, cw_ref, cb_ref, dtb_ref, an_ref, dsk_ref, nw_ref, s0_ref, c0_ref,
                y_ref, sout_ref, st_ref, xc_ref, u_ref, *, C, L):
    c = pl.program_id(1)
    hist = 8

    @pl.when(c == 0)
    def _():
        st_ref[...] = s0_ref[0]
        xc_ref[0:hist, :] = c0_ref[0]

    @pl.when(c > 0)
    def _():
        xc_ref[0:hist, :] = xc_ref[C:C + hist, :]

    xc_ref[hist:hist + C, :] = xbc_ref[...].astype(F32)

    for j in range(0, SSD_CONV_DIM, 512):
        sl = slice(j, j + 512)
        acc = cb_ref[:, sl] + xc_ref[hist:hist + C, sl] * cw_ref[3:4, sl]
        for w in range(SSD_CONV_W - 1):
            off = hist - (SSD_CONV_W - 1) + w
            acc = acc + xc_ref[off:off + C, sl] * cw_ref[w:w + 1, sl]
        u_ref[:, sl] = acc * _sigmoid(acc)

    row = _iota((C, LANES), 0) + c * C
    dt = jnp.where(row < L, _softplus(dt_ref[...] + dtb_ref[...]), 0.0)
    loga = dt * an_ref[...]
    ti = _iota((C, C), 0)
    si = _iota((C, C), 1)
    tril = si <= ti
    cum = jnp.dot(tril.astype(F32), loga, precision=HI, preferred_element_type=F32)
    eye = (_iota((LANES, LANES), 0) == _iota((LANES, LANES), 1)).astype(F32)
    cum_t = lax.dot_general(eye, cum, NT, precision=HI, preferred_element_type=F32)
    dt_t = lax.dot_general(eye, dt, NT, precision=HI, preferred_element_type=F32)
    w_t = dt_t * jnp.exp(cum_t[:, C - 1:C] - cum_t)
    e_last = jnp.exp(cum[C - 1:C, :])
    eye_b = eye.astype(BF16)
    lane = _iota((C, LANES), 1)
    lane_s = _iota((SSD_STATE, LANES), 1)
    lane_r = _iota((1, LANES), 1)
    hpg = SSD_HEADS // SSD_GROUPS

    for g in range(SSD_GROUPS):
        b_f = u_ref[:, SSD_INNER + g * SSD_STATE:SSD_INNER + (g + 1) * SSD_STATE]
        c_f = u_ref[:, SSD_INNER + (SSD_GROUPS + g) * SSD_STATE:SSD_INNER + (SSD_GROUPS + g + 1) * SSD_STATE]
        b_b = b_f.astype(BF16)
        cb = lax.dot_general(c_f.astype(BF16), b_b, NT, preferred_element_type=F32)
        b_t = lax.dot_general(eye_b, b_b, NT, preferred_element_type=F32)
        ypairs = []
        for jp in range(hpg // 2):
            p = g * (hpg // 2) + jp
            cs = slice(p * LANES, (p + 1) * LANES)
            xs_p = u_ref[:, cs]
            xs_b = xs_p.astype(BF16)
            st_p = st_ref[:, cs]
            st_b = st_p.astype(BF16)
            res, upd = [], []
            for h in (2 * p, 2 * p + 1):
                ccol = cum[:, h:h + 1]
                dec = jnp.where(tril, jnp.exp(jnp.minimum(ccol - cum_t[h:h + 1, :], 0.0)), 0.0)
                wmat = (cb * dec * dt_t[h:h + 1, :]).astype(BF16)
                c_e = (c_f * jnp.exp(ccol)).astype(BF16)
                res.append(jnp.dot(wmat, xs_b, preferred_element_type=F32)
                           + jnp.dot(c_e, st_b, preferred_element_type=F32))
                b_w = (b_t * w_t[h:h + 1, :]).astype(BF16)
                upd.append(jnp.dot(b_w, xs_b, preferred_element_type=F32))
            y_p = jnp.where(lane < SSD_HEAD_DIM, res[0], res[1]) + dsk_ref[:, cs] * xs_p
            e_p = jnp.where(lane_r < SSD_HEAD_DIM, e_last[:, 2 * p:2 * p + 1], e_last[:, 2 * p + 1:2 * p + 2])
            st_ref[:, cs] = st_p * e_p + jnp.where(lane_s < SSD_HEAD_DIM, upd[0], upd[1])
            ypairs.append(y_p)
        gs = slice(g * 512, (g + 1) * 512)
        yg = jnp.concatenate(ypairs, axis=1)
        zg = z_ref[:, gs].astype(F32)
        yg = yg * (zg * _sigmoid(zg))
        ms = jnp.mean(yg * yg, axis=-1, keepdims=True)
        y_ref[:, gs] = (yg * lax.rsqrt(ms + NORM_EPS) * nw_ref[:, gs]).astype(y_ref.dtype)

    @pl.when(c == pl.num_programs(1) - 1)
    def _():
        sout_ref[0] = st_ref[...]


def _ssd_scan(z, xbc, dt, cw, cb, dtb, an, dsk, nw, s0_t, c0, *, row_off, nseq, L, Lpad, C):
    rows = z.shape[0]
    nc = Lpad // C
    base = row_off // C

    def rmap(b, c):
        return (base + b * nc + c, 0)

    const = lambda b, c: (0, 0)
    in_specs = [
        pl.BlockSpec((C, SSD_INNER), rmap),
        pl.BlockSpec((C, SSD_CONV_DIM), rmap),
        pl.BlockSpec((C, LANES), rmap),
        pl.BlockSpec((SSD_CONV_W, SSD_CONV_DIM), const),
        pl.BlockSpec((1, SSD_CONV_DIM), const),
        pl.BlockSpec((1, LANES), const),
        pl.BlockSpec((1, LANES), const),
        pl.BlockSpec((1, SSD_INNER), const),
        pl.BlockSpec((1, SSD_INNER), const),
        pl.BlockSpec((1, SSD_STATE, SSD_INNER), lambda b, c: (b, 0, 0)),
        pl.BlockSpec((1, 8, SSD_CONV_DIM), lambda b, c: (b, 0, 0)),
    ]
    out_specs = [
        pl.BlockSpec((C, SSD_INNER), lambda b, c: (b * nc + c, 0)),
        pl.BlockSpec((1, SSD_STATE, SSD_INNER), lambda b, c: (b, 0, 0)),
    ]
    out_shape = [
        jax.ShapeDtypeStruct((nseq * Lpad, SSD_INNER), BF16),
        jax.ShapeDtypeStruct((nseq, SSD_STATE, SSD_INNER), F32),
    ]
    scratch = [
        pltpu.VMEM((SSD_STATE, SSD_INNER), F32),
        pltpu.VMEM((C + 8, SSD_CONV_DIM), F32),
        pltpu.VMEM((C, SSD_CONV_DIM), F32),
    ]
    del rows
    return pl.pallas_call(
        functools.partial(_ssd_kernel, C=C, L=L), grid=(nseq, nc), in_specs=in_specs, out_specs=out_specs,
        out_shape=out_shape, scratch_shapes=scratch,
        compiler_params=_params(("arbitrary", "arbitrary")))(z, xbc, dt, cw, cb, dtb, an, dsk, nw, s0_t, c0)


def _gla_kernel(q_ref, k_ref, v_ref, g_ref, lr_ref, w2_ref, b2_ref, nw_ref, s0_ref,
                o_ref, sout_ref, st_ref, *, C, L):
    c = pl.program_id(1)
    sub = GLA_SUB
    nb = C // sub

    @pl.when(c == 0)
    def _():
        st_ref[...] = s0_ref[0]

    row_k = _iota((C, GLA_KEY_DIM), 0) + c * C
    gk = jnp.dot(lr_ref[...].astype(BF16), w2_ref[...], preferred_element_type=F32) + b2_ref[...]
    logg = (jnp.minimum(gk, 0.0) - jnp.log1p(jnp.exp(-jnp.abs(gk)))) * (1.0 / GLA_GATE_NORM)
    logg = jnp.where(row_k < L, logg, 0.0)
    ti = _iota((C, C), 0)
    si = _iota((C, C), 1)
    tril = si <= ti
    cum_all = jnp.dot(tril.astype(F32), logg, precision=HI, preferred_element_type=F32)
    same_sub = (ti // sub) == (si // sub)
    diag_mask = jnp.logical_and(tril, same_sub)
    sel = ((_iota((sub * GLA_DK, C), 0) // GLA_DK) == (_iota((sub * GLA_DK, C), 1) % sub)).astype(BF16)
    row_v = _iota((C, GLA_DV), 0) + c * C
    valid_v = row_v < L
    scale = GLA_DK ** -0.5

    for h in range(GLA_HEADS):
        ks = slice(h * GLA_DK, (h + 1) * GLA_DK)
        vs = slice(h * GLA_DV, (h + 1) * GLA_DV)
        q = q_ref[:, ks].astype(F32) * scale
        k = k_ref[:, ks].astype(F32)
        v = jnp.where(valid_v, v_ref[:, vs].astype(F32), 0.0)
        v_b = v.astype(BF16)
        cum = cum_all[:, ks]
        st = st_ref[h * GLA_DV:(h + 1) * GLA_DV, :]
        q_e = (q * jnp.exp(cum)).astype(BF16)
        y = lax.dot_general(q_e, st.astype(BF16), NT, preferred_element_type=F32)
        if nb > 1:
            q_parts, k_parts = [], []
            for i in range(1, nb):
                r_i = cum[i * sub - 1:i * sub, :]
                qi = q[i * sub:(i + 1) * sub, :] * jnp.exp(cum[i * sub:(i + 1) * sub, :] - r_i)
                q_parts.append(jnp.concatenate(
                    [jnp.zeros((i * sub, GLA_DK), F32), qi, jnp.zeros((C - (i + 1) * sub, GLA_DK), F32)], axis=0)
                    if C - (i + 1) * sub > 0 else jnp.concatenate([jnp.zeros((i * sub, GLA_DK), F32), qi], axis=0))
                ki = k[0:i * sub, :] * jnp.exp(r_i - cum[0:i * sub, :])
                k_parts.append(jnp.concatenate([ki, jnp.zeros((C - i * sub, GLA_DK), F32)], axis=0))
            q_st = jnp.concatenate(q_parts, axis=1).astype(BF16)
            k_st = jnp.concatenate(k_parts, axis=1).astype(BF16)
            a_off = lax.dot_general(q_st, k_st, NT, preferred_element_type=F32)
        else:
            a_off = jnp.zeros((C, C), F32)
        q3 = q.reshape(nb, sub, GLA_DK)
        k3 = k.reshape(nb, sub, GLA_DK)
        c3 = cum.reshape(nb, sub, GLA_DK)
        p_parts = []
        for o in range(sub):
            p_o = q3 * jnp.exp(jnp.minimum(c3 - c3[:, o:o + 1, :], 0.0)) * k3[:, o:o + 1, :]
            p_parts.append(p_o.reshape(C, GLA_DK).astype(BF16))
        p_st = jnp.concatenate(p_parts, axis=1)
        a_diag = jnp.dot(p_st, sel, preferred_element_type=F32)
        a = a_off + jnp.where(diag_mask, a_diag, 0.0)
        y = y + jnp.dot(a.astype(BF16), v_b, preferred_element_type=F32)
        last = cum[C - 1:C, :]
        k_e = (k * jnp.exp(last - cum)).astype(BF16)
        st_ref[h * GLA_DV:(h + 1) * GLA_DV, :] = (
            st * jnp.exp(last) + lax.dot_general(v_b, k_e, TN, preferred_element_type=F32))
        ms = jnp.mean(y * y, axis=-1, keepdims=True)
        gate = g_ref[:, vs].astype(F32)
        o_ref[:, vs] = (y * lax.rsqrt(ms + NORM_EPS) * nw_ref[...] * (gate * _sigmoid(gate))).astype(o_ref.dtype)

    @pl.when(c == pl.num_programs(1) - 1)
    def _():
        sout_ref[0] = st_ref[...]


def _gla_scan(qkvg, lr, w2, b2, nw, s0_t, *, row_off, nseq, L, Lpad, C):
    nc = Lpad // C
    base = row_off // C
    const = lambda b, c: (0, 0)
    in_specs = [
        pl.BlockSpec((C, GLA_KEY_DIM), lambda b, c: (base + b * nc + c, 0)),
        pl.BlockSpec((C, GLA_KEY_DIM), lambda b, c: (base + b * nc + c, 1)),
        pl.BlockSpec((C, GLA_VAL_DIM), lambda b, c: (base + b * nc + c, 1)),
        pl.BlockSpec((C, GLA_VAL_DIM), lambda b, c: (base + b * nc + c, 2)),
        pl.BlockSpec((C, LANES), lambda b, c: (base + b * nc + c, 0)),
        pl.BlockSpec((LANES, GLA_KEY_DIM), const),
        pl.BlockSpec((1, GLA_KEY_DIM), const),
        pl.BlockSpec((1, GLA_DV), const),
        pl.BlockSpec((1, GLA_VAL_DIM, GLA_DK), lambda b, c: (b, 0, 0)),
    ]
    out_specs = [
        pl.BlockSpec((C, GLA_VAL_DIM), lambda b, c: (b * nc + c, 0)),
        pl.BlockSpec((1, GLA_VAL_DIM, GLA_DK), lambda b, c: (b, 0, 0)),
    ]
    out_shape = [
        jax.ShapeDtypeStruct((nseq * Lpad, GLA_VAL_DIM), BF16),
        jax.ShapeDtypeStruct((nseq, GLA_VAL_DIM, GLA_DK), F32),
    ]
    return pl.pallas_call(
        functools.partial(_gla_kernel, C=C, L=L), grid=(nseq, nc), in_specs=in_specs, out_specs=out_specs,
        out_shape=out_shape, scratch_shapes=[pltpu.VMEM((GLA_VAL_DIM, GLA_DK), F32)],
        compiler_params=_params(("arbitrary", "arbitrary")))(qkvg, qkvg, qkvg, qkvg, lr, w2, b2, nw, s0_t)


def _layer_norm(x, g, b):
    mu = jnp.mean(x, axis=-1, keepdims=True)
    xc = x - mu
    var = jnp.mean(xc * xc, axis=-1, keepdims=True)
    return xc * lax.rsqrt(var + NORM_EPS) * g + b


def _post_kernel(y_ref, wo_ref, h_ref, g_ref, b_ref, wr_ref, br_ref,
                 h1_ref, ri_ref, rw_ref, cnt_ref, carry_ref, *, alpha):
    i = pl.program_id(0)
    tm = h_ref.shape[0]

    @pl.when(i == 0)
    def _():
        carry_ref[...] = jnp.zeros_like(carry_ref)

    mix = jnp.dot(y_ref[...], wo_ref[...], preferred_element_type=F32)
    h1 = _layer_norm(alpha * h_ref[...] + mix, g_ref[...], b_ref[...])
    h1_ref[...] = h1
    logits = jnp.dot(h1, wr_ref[...], precision=HI, preferred_element_type=F32) + br_ref[...]
    lane = _iota((tm, LANES), 1)
    neg = -jnp.inf
    is_g = lane < MOE_GROUPS
    gl = jnp.where(is_g, logits, neg)
    gmax = jnp.max(gl, axis=-1, keepdims=True)
    gsel = jnp.min(jnp.where(gl == gmax, lane, LANES), axis=-1, keepdims=True)
    p_g = 1.0 / jnp.sum(jnp.where(is_g, jnp.exp(gl - gmax), 0.0), axis=-1, keepdims=True)
    eid = lane - MOE_GROUPS
    in_g = (eid >= 0) & (eid < MOE_EXPERTS) & ((eid // MOE_EXP_PER_GROUP) == gsel)
    el = jnp.where(in_g, logits, neg)
    v1 = jnp.max(el, axis=-1, keepdims=True)
    i1 = jnp.min(jnp.where(el == v1, lane, LANES), axis=-1, keepdims=True)
    el2 = jnp.where(lane == i1, neg, el)
    v2 = jnp.max(el2, axis=-1, keepdims=True)
    i2 = jnp.min(jnp.where(el2 == v2, lane, LANES), axis=-1, keepdims=True)
    t = jnp.exp(v2 - v1)
    w1 = p_g / (1.0 + t)
    w2 = p_g * t / (1.0 + t)
    e1 = i1 - MOE_GROUPS
    e2 = i2 - MOE_GROUPS
    oh = jnp.where(lane == e1, 1.0, 0.0) + jnp.where(lane == e2, 1.0, 0.0)
    strict = (_iota((tm, tm), 1) < _iota((tm, tm), 0)).astype(BF16)
    before = jnp.dot(strict, oh.astype(BF16), preferred_element_type=F32) + carry_ref[...]
    r1 = jnp.sum(jnp.where(lane == e1, before, 0.0), axis=-1, keepdims=True)
    r2 = jnp.sum(jnp.where(lane == e2, before, 0.0), axis=-1, keepdims=True)
    carry_ref[...] = carry_ref[...] + jnp.sum(oh, axis=0, keepdims=True)
    cnt_ref[...] = carry_ref[...]
    ri_ref[...] = jnp.where(lane == 0, e1, jnp.where(lane == 1, e2, jnp.where(
        lane == 2, r1.astype(I32), jnp.where(lane == 3, r2.astype(I32), 0))))
    rw_ref[...] = jnp.where(lane == 0, w1, jnp.where(lane == 1, w2, 0.0))


def _post_mixer(y, wo, h, g, b, wr, br, *, alpha):
    rows, kin = y.shape
    grid = (rows // ROW_TILE,)
    rmap = lambda i: (i, 0)
    const = lambda i: (0, 0)
    in_specs = [
        pl.BlockSpec((ROW_TILE, kin), rmap),
        pl.BlockSpec((kin, D_MODEL), const),
        pl.BlockSpec((ROW_TILE, D_MODEL), rmap),
        pl.BlockSpec((1, D_MODEL), const),
        pl.BlockSpec((1, D_MODEL), const),
        pl.BlockSpec((D_MODEL, LANES), const),
        pl.BlockSpec((1, LANES), const),
    ]
    out_specs = [
        pl.BlockSpec((ROW_TILE, D_MODEL), rmap),
        pl.BlockSpec((ROW_TILE, LANES), rmap),
        pl.BlockSpec((ROW_TILE, LANES), rmap),
        pl.BlockSpec((1, LANES), const),
    ]
    out_shape = [
        jax.ShapeDtypeStruct((rows, D_MODEL), F32),
        jax.ShapeDtypeStruct((rows, LANES), I32),
        jax.ShapeDtypeStruct((rows, LANES), F32),
        jax.ShapeDtypeStruct((1, LANES), F32),
    ]
    return pl.pallas_call(
        functools.partial(_post_kernel, alpha=alpha), grid=grid, in_specs=in_specs, out_specs=out_specs,
        out_shape=out_shape, scratch_shapes=[pltpu.VMEM((1, LANES), F32)],
        compiler_params=_params(("arbitrary",)))(y, wo, h, g, b, wr, br)


def _expert_kernel(be_ref, nu_ref, x_ref, wg_ref, wu_ref, wd_ref, o_ref, wg_b, wu_b, wd_b):
    i = pl.program_id(0)
    prev = be_ref[jnp.maximum(i - 1, 0)]

    @pl.when(jnp.logical_or(i == 0, be_ref[i] != prev))
    def _():
        wg_b[...] = wg_ref[...].astype(BF16)
        wu_b[...] = wu_ref[...].astype(BF16)
        wd_b[...] = wd_ref[...].astype(BF16)

    @pl.when(i < nu_ref[0])
    def _():
        x = x_ref[...]
        hg = jnp.dot(x, wg_b[...], preferred_element_type=F32)
        hu = jnp.dot(x, wu_b[...], preferred_element_type=F32)
        hh = (hg * _sigmoid(hg) * hu).astype(BF16)
        o_ref[...] = jnp.dot(hh, wd_b[...], preferred_element_type=F32).astype(o_ref.dtype)

    @pl.when(i >= nu_ref[0])
    def _():
        o_ref[...] = jnp.zeros_like(o_ref)


def _experts(block_e, n_used, xb, w_gate, w_up, w_down, layer):
    nblk = xb.shape[0] // EXPERT_BLOCK
    wmap = lambda i, be, nu: (layer, be[i], 0, 0)
    grid_spec = pltpu.PrefetchScalarGridSpec(
        num_scalar_prefetch=2, grid=(nblk,),
        in_specs=[
            pl.BlockSpec((EXPERT_BLOCK, D_MODEL), lambda i, be, nu: (i, 0)),
            pl.BlockSpec((None, None, D_MODEL, EXPERT_FF), wmap),
            pl.BlockSpec((None, None, D_MODEL, EXPERT_FF), wmap),
            pl.BlockSpec((None, None, EXPERT_FF, D_MODEL), wmap),
        ],
        out_specs=pl.BlockSpec((EXPERT_BLOCK, D_MODEL), lambda i, be, nu: (i, 0)),
        scratch_shapes=[pltpu.VMEM((D_MODEL, EXPERT_FF), BF16), pltpu.VMEM((D_MODEL, EXPERT_FF), BF16),
                        pltpu.VMEM((EXPERT_FF, D_MODEL), BF16)])
    return pl.pallas_call(
        _expert_kernel, grid_spec=grid_spec,
        out_shape=jax.ShapeDtypeStruct((nblk * EXPERT_BLOCK, D_MODEL), F32),
        compiler_params=_params(("arbitrary",)))(block_e, n_used, xb, w_gate, w_up, w_down)


def _combine_kernel(h_ref, ya_ref, yb_ref, rw_ref, g_ref, b_ref, o_ref, *, alpha):
    rw = rw_ref[...]
    ffn = ya_ref[...].astype(F32) * rw[:, 0:1] + yb_ref[...].astype(F32) * rw[:, 1:2]
    o_ref[...] = _layer_norm(alpha * h_ref[...] + ffn, g_ref[...], b_ref[...])


def _combine(h1, ya, yb, rw, g, b, *, alpha):
    rows = h1.shape[0]
    rmap = lambda i: (i, 0)
    const = lambda i: (0, 0)
    in_specs = [
        pl.BlockSpec((ROW_TILE, D_MODEL), rmap), pl.BlockSpec((ROW_TILE, D_MODEL), rmap),
        pl.BlockSpec((ROW_TILE, D_MODEL), rmap), pl.BlockSpec((ROW_TILE, LANES), rmap),
        pl.BlockSpec((1, D_MODEL), const), pl.BlockSpec((1, D_MODEL), const),
    ]
    return pl.pallas_call(
        functools.partial(_combine_kernel, alpha=alpha), grid=(rows // ROW_TILE,), in_specs=in_specs,
        out_specs=pl.BlockSpec((ROW_TILE, D_MODEL), rmap),
        out_shape=jax.ShapeDtypeStruct((rows, D_MODEL), F32),
        compiler_params=_params(("parallel",)))(h1, ya, yb, rw, g, b)


def _round_up(x, m):
    return (x + m - 1) // m * m


def _pad_cols(w, n):
    return jnp.pad(w, ((0, 0), (0, n - w.shape[1])))


def _moe(h1, ri, rw, cnt, w_gate, w_up, w_down, layer):
    rows = h1.shape[0]
    nblk = (2 * rows + MOE_EXPERTS * (EXPERT_BLOCK - 1) + EXPERT_BLOCK - 1) // EXPERT_BLOCK
    counts = cnt[0, :MOE_EXPERTS].astype(I32)
    pcounts = (counts + EXPERT_BLOCK - 1) // EXPERT_BLOCK * EXPERT_BLOCK
    pends = jnp.cumsum(pcounts)
    pstarts = pends - pcounts
    dest = pstarts[ri[:, 0:2]] + ri[:, 2:4]
    tok = jnp.broadcast_to(jnp.arange(rows, dtype=I32)[:, None], (rows, 2))
    row_tok = jnp.zeros((nblk * EXPERT_BLOCK,), I32).at[dest.reshape(-1)].set(tok.reshape(-1))
    block_e = jnp.minimum(
        jnp.searchsorted(pends, jnp.arange(nblk, dtype=I32) * EXPERT_BLOCK, side='right'), MOE_EXPERTS - 1).astype(I32)
    n_used = (pends[-1:] // EXPERT_BLOCK).astype(I32)
    xb = h1.astype(BF16)[row_tok]
    yb = _experts(block_e, n_used, xb, w_gate, w_up, w_down, layer)
    return yb[dest[:, 0]], yb[dest[:, 1]]


def kernel(x_prompt, x_sample, state_ssd, state_ssd_conv, state_gla, meta_tokens, ssd_w_in, ssd_conv_w, ssd_conv_b, ssd_dt_bias, ssd_A_log, ssd_D, ssd_norm_w, ssd_w_out, gla_w_in, gla_w_gk2, gla_b_gk2, gla_norm_w, gla_w_out, ln1_g, ln1_b, moe_w_grp, moe_b_grp, moe_w_exp, moe_b_exp, moe_w_gate, moe_w_up, moe_w_down, ln2_g, ln2_b):
    bp, seq, _ = x_prompt.shape
    bs, lsm, _ = x_sample.shape
    depth = ln1_g.shape[0]
    alpha = (2.0 * depth) ** 0.25
    lp = N_META + seq
    cp = PROMPT_CHUNK
    lp_pad = _round_up(lp, cp)
    cs = lsm
    off_s = bp * lp_pad
    assert off_s % cs == 0 and cs % GLA_SUB == 0 and cp % GLA_SUB == 0
    rows = _round_up(off_s + bs * lsm, ROW_TILE)

    meta = jnp.broadcast_to(meta_tokens.astype(F32)[None], (bp, N_META, D_MODEL))
    hp = jnp.concatenate([meta, x_prompt, jnp.zeros((bp, lp_pad - lp, D_MODEL), F32)], axis=1)
    h = jnp.concatenate([hp.reshape(bp * lp_pad, D_MODEL), x_sample.reshape(bs * lsm, D_MODEL),
                         jnp.zeros((rows - off_s - bs * lsm, D_MODEL), F32)], axis=0)

    groups = (dict(row_off=0, nseq=bp, L=lp, Lpad=lp_pad, C=cp),
              dict(row_off=off_s, nseq=bs, L=lsm, Lpad=lsm, C=cs))

    def scatter_rows(parts):
        tail = rows - off_s - bs * lsm
        pieces = list(parts)
        if tail:
            pieces.append(jnp.zeros((tail, parts[0].shape[1]), parts[0].dtype))
        return jnp.concatenate(pieces, axis=0)

    new_ssd_p, new_ssd_s, new_conv_p, new_conv_s, new_gla_p, new_gla_s = [], [], [], [], [], []
    for i in range(depth):
        j = i // 2
        if i % 2 == 0:
            w_in = ssd_w_in[j].astype(BF16)
            z, xbc, dtr = _proj(
                h, [w_in[:, :SSD_INNER], w_in[:, SSD_INNER:SSD_INNER + SSD_CONV_DIM],
                    _pad_cols(w_in[:, SSD_INNER + SSD_CONV_DIM:], LANES)], [F32, F32, F32])
            cw = ssd_conv_w[j]
            cb = ssd_conv_b[j][None]
            dtb = _pad_cols(ssd_dt_bias[j][None], LANES)
            an = _pad_cols(-jnp.exp(ssd_A_log[j].astype(F32))[None], LANES)
            dsk = jnp.repeat(ssd_D[j], SSD_HEAD_DIM)[None]
            nw = ssd_norm_w[j][None]
            s0s = [jnp.zeros((bp, SSD_STATE, SSD_INNER), F32),
                   jnp.swapaxes(state_ssd[j].reshape(bs, SSD_INNER, SSD_STATE), 1, 2)]
            c0s = [jnp.zeros((bp, 8, SSD_CONV_DIM), F32),
                   jnp.pad(state_ssd_conv[j], ((0, 0), (8 - (SSD_CONV_W - 1), 0), (0, 0)))]
            ys = []
            for grp, s0, c0, acc_s, acc_c in zip(groups, s0s, c0s, (new_ssd_p, new_ssd_s), (new_conv_p, new_conv_s)):
                y, st = _ssd_scan(z, xbc, dtr, cw, cb, dtb, an, dsk, nw, s0, c0, **grp)
                ys.append(y)
                n, L, Lpad = grp['nseq'], grp['L'], grp['Lpad']
                acc_s.append(jnp.swapaxes(st, 1, 2).reshape(n, SSD_HEADS, SSD_HEAD_DIM, SSD_STATE))
                xg = lax.slice_in_dim(xbc, grp['row_off'], grp['row_off'] + n * Lpad, axis=0)
                acc_c.append(xg.reshape(n, Lpad, SSD_CONV_DIM)[:, L - (SSD_CONV_W - 1):L])
            ymix = scatter_rows(ys)
            wo = ssd_w_out[j].astype(BF16)
        else:
            w_in = gla_w_in[j].astype(BF16)
            nq = 2 * GLA_KEY_DIM + 2 * GLA_VAL_DIM
            qkvg, lr = _proj(h, [w_in[:, :nq], _pad_cols(w_in[:, nq:], LANES)], [F32, F32])
            w2 = jnp.pad(gla_w_gk2[j], ((0, LANES - GLA_GATE_RANK), (0, 0))).astype(BF16)
            b2 = gla_b_gk2[j][None]
            nw = gla_norm_w[j][None]
            s0s = [jnp.zeros((bp, GLA_VAL_DIM, GLA_DK), F32),
                   jnp.swapaxes(state_gla[j], 2, 3).reshape(bs, GLA_VAL_DIM, GLA_DK)]
            ys = []
            for grp, s0, acc in zip(groups, s0s, (new_gla_p, new_gla_s)):
                y, st = _gla_scan(qkvg, lr, w2, b2, nw, s0, **grp)
                ys.append(y)
                acc.append(jnp.swapaxes(st.reshape(grp['nseq'], GLA_HEADS, GLA_DV, GLA_DK), 2, 3))
            ymix = scatter_rows(ys)
            wo = gla_w_out[j].astype(BF16)

        wr = _pad_cols(jnp.concatenate([moe_w_grp[i], moe_w_exp[i]], axis=1), LANES)
        br = _pad_cols(jnp.concatenate([moe_b_grp[i], moe_b_exp[i]])[None], LANES)
        h1, ri, rw, cnt = _post_mixer(ymix, wo, h, ln1_g[i][None], ln1_b[i][None], wr, br, alpha=alpha)
        ya, yb = _moe(h1, ri, rw, cnt, moe_w_gate, moe_w_up, moe_w_down, i)
        h = _combine(h1, ya, yb, rw, ln2_g[i][None], ln2_b[i][None], alpha=alpha)

    y_prompt = h[:off_s].reshape(bp, lp_pad, D_MODEL)[:, N_META:lp]
    y_sample = h[off_s:off_s + bs * lsm].reshape(bs, lsm, D_MODEL)
    return (y_prompt, y_sample, jnp.stack(new_ssd_p), jnp.stack(new_conv_p), jnp.stack(new_gla_p),
            jnp.stack(new_ssd_s), jnp.stack(new_conv_s), jnp.stack(new_gla_s))
```

```python
import functools
import math

import jax
import jax.numpy as jnp
from jax import lax
from jax.experimental import pallas as pl
from jax.experimental.pallas import tpu as pltpu

F32 = jnp.float32
BF16 = jnp.bfloat16
I32 = jnp.int32
HI = lax.Precision.HIGHEST
NT = (((1,), (1,)), ((), ()))
TN = (((0,), (0,)), ((), ()))

D_MODEL = 1024
N_META = 16
SSD_INNER = 2048
SSD_HEAD_DIM = 64
SSD_HEADS = 32
SSD_GROUPS = 4
SSD_STATE = 128
SSD_CONV_W = 4
SSD_CONV_DIM = 3072
GLA_HEADS = 4
GLA_KEY_DIM = 512
GLA_VAL_DIM = 1024
GLA_DK = 128
GLA_DV = 256
GLA_GATE_RANK = 16
GLA_GATE_NORM = 16.0
MOE_GROUPS = 4
MOE_EXP_PER_GROUP = 8
MOE_EXPERTS = 32
EXPERT_FF = 512
NORM_EPS = 1e-5

LANES = 128
VMEM_LIMIT_MB = 56
ROW_TILE = 256
POST_TILE = 512
PROMPT_CHUNK = 128
GLA_SUB = 16
EXPERT_BLOCK = 256


def _params(sem):
    return pltpu.CompilerParams(dimension_semantics=sem, vmem_limit_bytes=VMEM_LIMIT_MB << 20)


def _sigmoid(x):
    return 1.0 / (1.0 + jnp.exp(-x))


def _softplus(x):
    return jnp.maximum(x, 0.0) + jnp.log1p(jnp.exp(-jnp.abs(x)))


def _iota(shape, dim):
    return lax.broadcasted_iota(I32, shape, dim)


def _proj_kernel(x_ref, *refs):
    n = len(refs) // 2
    xb = x_ref[...].astype(BF16)
    for w_ref, o_ref in zip(refs[:n], refs[n:]):
        o_ref[...] = jnp.dot(xb, w_ref[...], preferred_element_type=F32).astype(o_ref.dtype)


def _proj(x, ws, out_dtypes):
    rows, k = x.shape
    grid = (rows // ROW_TILE,)
    in_specs = [pl.BlockSpec((ROW_TILE, k), lambda i: (i, 0))]
    in_specs += [pl.BlockSpec(w.shape, lambda i: (0, 0)) for w in ws]
    out_specs = [pl.BlockSpec((ROW_TILE, w.shape[1]), lambda i: (i, 0)) for w in ws]
    out_shape = [jax.ShapeDtypeStruct((rows, w.shape[1]), dt) for w, dt in zip(ws, out_dtypes)]
    return pl.pallas_call(
        _proj_kernel, grid=grid, in_specs=in_specs, out_specs=out_specs, out_shape=out_shape,
        compiler_params=_params(("parallel",)))(x, *ws)


def _ssd_kernel(z_ref, xbc_ref, dt_ref, cw_ref, cb_ref, dtb_ref, an_ref, dsk_ref, nw_ref, s0_ref, c0_ref,
                y_ref, sout_ref, st_ref, xc_ref, u_ref, *, C, L):
    c = pl.program_id(1)
    hist = 8

    @pl.when(c == 0)
    def _():
        st_ref[...] = s0_ref[0]
        xc_ref[0:hist, :] = c0_ref[0]

    @pl.when(c > 0)
    def _():
        xc_ref[0:hist, :] = xc_ref[C:C + hist, :]

    xc_ref[hist:hist + C, :] = xbc_ref[...].astype(F32)

    for j in range(0, SSD_CONV_DIM, 512):
        sl = slice(j, j + 512)
        acc = cb_ref[:, sl] + xc_ref[hist:hist + C, sl] * cw_ref[3:4, sl]
        for w in range(SSD_CONV_W - 1):
            off = hist - (SSD_CONV_W - 1) + w
            acc = acc + xc_ref[off:off + C, sl] * cw_ref[w:w + 1, sl]
        u_ref[:, sl] = acc * _sigmoid(acc)

    row = _iota((C, LANES), 0) + c * C
    dt = jnp.where(row < L, _softplus(dt_ref[...] + dtb_ref[...]), 0.0)
    loga = dt * an_ref[...]
    ti = _iota((C, C), 0)
    si = _iota((C, C), 1)
    tril = si <= ti
    cum = jnp.dot(tril.astype(F32), loga, precision=HI, preferred_element_type=F32)
    eye = (_iota((LANES, LANES), 0) == _iota((LANES, LANES), 1)).astype(F32)
    cum_t = lax.dot_general(eye, cum, NT, precision=HI, preferred_element_type=F32)
    dt_t = lax.dot_general(eye, dt, NT, precision=HI, preferred_element_type=F32)
    w_t = dt_t * jnp.exp(cum_t[:, C - 1:C] - cum_t)
    e_last = jnp.exp(cum[C - 1:C, :])
    eye_b = eye.astype(BF16)
    lane = _iota((C, LANES), 1)
    lane_s = _iota((SSD_STATE, LANES), 1)
    lane_r = _iota((1, LANES), 1)
    hpg = SSD_HEADS // SSD_GROUPS

    for g in range(SSD_GROUPS):
        b_f = u_ref[:, SSD_INNER + g * SSD_STATE:SSD_INNER + (g + 1) * SSD_STATE]
        c_f = u_ref[:, SSD_INNER + (SSD_GROUPS + g) * SSD_STATE:SSD_INNER + (SSD_GROUPS + g + 1) * SSD_STATE]
        b_b = b_f.astype(BF16)
        cb = lax.dot_general(c_f.astype(BF16), b_b, NT, preferred_element_type=F32)
        b_t = lax.dot_general(eye_b, b_b, NT, preferred_element_type=F32)
        ypairs = []
        for jp in range(hpg // 2):
            p = g * (hpg // 2) + jp
            cs = slice(p * LANES, (p + 1) * LANES)
            xs_p = u_ref[:, cs]
            xs_b = xs_p.astype(BF16)
            st_p = st_ref[:, cs]
            st_b = st_p.astype(BF16)
            res, upd = [], []
            for h in (2 * p, 2 * p + 1):
                ccol = cum[:, h:h + 1]
                dec = jnp.where(tril, jnp.exp(jnp.minimum(ccol - cum_t[h:h + 1, :], 0.0)), 0.0)
                wmat = (cb * dec * dt_t[h:h + 1, :]).astype(BF16)
                c_e = (c_f * jnp.exp(ccol)).astype(BF16)
                res.append(jnp.dot(wmat, xs_b, preferred_element_type=F32)
                           + jnp.dot(c_e, st_b, preferred_element_type=F32))
                b_w = (b_t * w_t[h:h + 1, :]).astype(BF16)
                upd.append(jnp.dot(b_w, xs_b, preferred_element_type=F32))
            y_p = jnp.where(lane < SSD_HEAD_DIM, res[0], res[1]) + dsk_ref[:, cs] * xs_p
            e_p = jnp.where(lane_r < SSD_HEAD_DIM, e_last[:, 2 * p:2 * p + 1], e_last[:, 2 * p + 1:2 * p + 2])
            st_ref[:, cs] = st_p * e_p + jnp.where(lane_s < SSD_HEAD_DIM, upd[0], upd[1])
            ypairs.append(y_p)
        gs = slice(g * 512, (g + 1) * 512)
        yg = jnp.concatenate(ypairs, axis=1)
        zg = z_ref[:, gs].astype(F32)
        yg = yg * (zg * _sigmoid(zg))
        ms = jnp.mean(yg * yg, axis=-1, keepdims=True)
        y_ref[:, gs] = (yg * lax.rsqrt(ms + NORM_EPS) * nw_ref[:, gs]).astype(y_ref.dtype)

    @pl.when(c == pl.num_programs(1) - 1)
    def _():
        sout_ref[0] = st_ref[...]


def _ssd_scan(z, xbc, dt, cw, cb, dtb, an, dsk, nw, s0_t, c0, *, row_off, nseq, L, Lpad, C):
    rows = z.shape[0]
    nc = Lpad // C
    base = row_off // C

    def rmap(b, c):
        return (base + b * nc + c, 0)

    const = lambda b, c: (0, 0)
    in_specs = [
        pl.BlockSpec((C, SSD_INNER), rmap),
        pl.BlockSpec((C, SSD_CONV_DIM), rmap),
        pl.BlockSpec((C, LANES), rmap),
        pl.BlockSpec((SSD_CONV_W, SSD_CONV_DIM), const),
        pl.BlockSpec((1, SSD_CONV_DIM), const),
        pl.BlockSpec((1, LANES), const),
        pl.BlockSpec((1, LANES), const),
        pl.BlockSpec((1, SSD_INNER), const),
        pl.BlockSpec((1, SSD_INNER), const),
        pl.BlockSpec((1, SSD_STATE, SSD_INNER), lambda b, c: (b, 0, 0)),
        pl.BlockSpec((1, 8, SSD_CONV_DIM), lambda b, c: (b, 0, 0)),
    ]
    out_specs = [
        pl.BlockSpec((C, SSD_INNER), lambda b, c: (b * nc + c, 0)),
        pl.BlockSpec((1, SSD_STATE, SSD_INNER), lambda b, c: (b, 0, 0)),
    ]
    out_shape = [
        jax.ShapeDtypeStruct((nseq * Lpad, SSD_INNER), BF16),
        jax.ShapeDtypeStruct((nseq, SSD_STATE, SSD_INNER), F32),
    ]
    scratch = [
        pltpu.VMEM((SSD_STATE, SSD_INNER), F32),
        pltpu.VMEM((C + 8, SSD_CONV_DIM), F32),
        pltpu.VMEM((C, SSD_CONV_DIM), F32),
    ]
    del rows
    return pl.pallas_call(
        functools.partial(_ssd_kernel, C=C, L=L), grid=(nseq, nc), in_specs=in_specs, out_specs=out_specs,
        out_shape=out_shape, scratch_shapes=scratch,
        compiler_params=_params(("arbitrary", "arbitrary")))(z, xbc, dt, cw, cb, dtb, an, dsk, nw, s0_t, c0)


def _gla_kernel(q_ref, k_ref, v_ref, g_ref, lr_ref, w2_ref, b2_ref, nw_ref, s0_ref,
                o_ref, sout_ref, st_ref, *, C, L):
    c = pl.program_id(1)
    sub = GLA_SUB
    nb = C // sub

    @pl.when(c == 0)
    def _():
        st_ref[...] = s0_ref[0]

    row_k = _iota((C, GLA_KEY_DIM), 0) + c * C
    gk = jnp.dot(lr_ref[...].astype(BF16), w2_ref[...], preferred_element_type=F32) + b2_ref[...]
    logg = (jnp.minimum(gk, 0.0) - jnp.log1p(jnp.exp(-jnp.abs(gk)))) * (1.0 / GLA_GATE_NORM)
    logg = jnp.where(row_k < L, logg, 0.0)
    ti = _iota((C, C), 0)
    si = _iota((C, C), 1)
    tril = si <= ti
    cum_all = jnp.dot(tril.astype(F32), logg, precision=HI, preferred_element_type=F32)
    same_sub = (ti // sub) == (si // sub)
    diag_mask = jnp.logical_and(tril, same_sub)
    sel = ((_iota((sub * GLA_DK, C), 0) // GLA_DK) == (_iota((sub * GLA_DK, C), 1) % sub)).astype(BF16)
    row_v = _iota((C, GLA_DV), 0) + c * C
    valid_v = row_v < L
    scale = GLA_DK ** -0.5

    for h in range(GLA_HEADS):
        ks = slice(h * GLA_DK, (h + 1) * GLA_DK)
        vs = slice(h * GLA_DV, (h + 1) * GLA_DV)
        q = q_ref[:, ks].astype(F32) * scale
        k = k_ref[:, ks].astype(F32)
        v = jnp.where(valid_v, v_ref[:, vs].astype(F32), 0.0)
        v_b = v.astype(BF16)
        cum = cum_all[:, ks]
        st = st_ref[h * GLA_DV:(h + 1) * GLA_DV, :]
        q_e = (q * jnp.exp(cum)).astype(BF16)
        y = lax.dot_general(q_e, st.astype(BF16), NT, preferred_element_type=F32)
        if nb > 1:
            q_parts, k_parts = [], []
            for i in range(1, nb):
                r_i = cum[i * sub - 1:i * sub, :]
                qi = q[i * sub:(i + 1) * sub, :] * jnp.exp(cum[i * sub:(i + 1) * sub, :] - r_i)
                q_parts.append(jnp.concatenate(
                    [jnp.zeros((i * sub, GLA_DK), F32), qi, jnp.zeros((C - (i + 1) * sub, GLA_DK), F32)], axis=0)
                    if C - (i + 1) * sub > 0 else jnp.concatenate([jnp.zeros((i * sub, GLA_DK), F32), qi], axis=0))
                ki = k[0:i * sub, :] * jnp.exp(r_i - cum[0:i * sub, :])
                k_parts.append(jnp.concatenate([ki, jnp.zeros((C - i * sub, GLA_DK), F32)], axis=0))
            q_st = jnp.concatenate(q_parts, axis=1).astype(BF16)
            k_st = jnp.concatenate(k_parts, axis=1).astype(BF16)
            a_off = lax.dot_general(q_st, k_st, NT, preferred_element_type=F32)
        else:
            a_off = jnp.zeros((C, C), F32)
        q3 = q.reshape(nb, sub, GLA_DK)
        k3 = k.reshape(nb, sub, GLA_DK)
        c3 = cum.reshape(nb, sub, GLA_DK)
        p_parts = []
        for o in range(sub):
            p_o = q3 * jnp.exp(jnp.minimum(c3 - c3[:, o:o + 1, :], 0.0)) * k3[:, o:o + 1, :]
            p_parts.append(p_o.reshape(C, GLA_DK).astype(BF16))
        p_st = jnp.concatenate(p_parts, axis=1)
        a_diag = jnp.dot(p_st, sel, preferred_element_type=F32)
        a = a_off + jnp.where(diag_mask, a_diag, 0.0)
        y = y + jnp.dot(a.astype(BF16), v_b, preferred_element_type=F32)
        last = cum[C - 1:C, :]
        k_e = (k * jnp.exp(last - cum)).astype(BF16)
        st_ref[h * GLA_DV:(h + 1) * GLA_DV, :] = (
            st * jnp.exp(last) + lax.dot_general(v_b, k_e, TN, preferred_element_type=F32))
        ms = jnp.mean(y * y, axis=-1, keepdims=True)
        gate = g_ref[:, vs].astype(F32)
        o_ref[:, vs] = (y * lax.rsqrt(ms + NORM_EPS) * nw_ref[...] * (gate * _sigmoid(gate))).astype(o_ref.dtype)

    @pl.when(c == pl.num_programs(1) - 1)
    def _():
        sout_ref[0] = st_ref[...]


def _gla_scan(qkvg, lr, w2, b2, nw, s0_t, *, row_off, nseq, L, Lpad, C):
    nc = Lpad // C
    base = row_off // C
    const = lambda b, c: (0, 0)
    in_specs = [
        pl.BlockSpec((C, GLA_KEY_DIM), lambda b, c: (base + b * nc + c, 0)),
        pl.BlockSpec((C, GLA_KEY_DIM), lambda b, c: (base + b * nc + c, 1)),
        pl.BlockSpec((C, GLA_VAL_DIM), lambda b, c: (base + b * nc + c, 1)),
        pl.BlockSpec((C, GLA_VAL_DIM), lambda b, c: (base + b * nc + c, 2)),
        pl.BlockSpec((C, LANES), lambda b, c: (base + b * nc + c, 0)),
        pl.BlockSpec((LANES, GLA_KEY_DIM), const),
        pl.BlockSpec((1, GLA_KEY_DIM), const),
        pl.BlockSpec((1, GLA_DV), const),
        pl.BlockSpec((1, GLA_VAL_DIM, GLA_DK), lambda b, c: (b, 0, 0)),
    ]
    out_specs = [
        pl.BlockSpec((C, GLA_VAL_DIM), lambda b, c: (b * nc + c, 0)),
        pl.BlockSpec((1, GLA_VAL_DIM, GLA_DK), lambda b, c: (b, 0, 0)),
    ]
    out_shape = [
        jax.ShapeDtypeStruct((nseq * Lpad, GLA_VAL_DIM), BF16),
        jax.ShapeDtypeStruct((nseq, GLA_VAL_DIM, GLA_DK), F32),
    ]
    return pl.pallas_call(
        functools.partial(_gla_kernel, C=C, L=L), grid=(nseq, nc), in_specs=in_specs, out_specs=out_specs,
        out_shape=out_shape, scratch_shapes=[pltpu.VMEM((GLA_VAL_DIM, GLA_DK), F32)],
        compiler_params=_params(("arbitrary", "arbitrary")))(qkvg, qkvg, qkvg, qkvg, lr, w2, b2, nw, s0_t)


def _layer_norm(x, g, b):
    mu = jnp.mean(x, axis=-1, keepdims=True)
    xc = x - mu
    var = jnp.mean(xc * xc, axis=-1, keepdims=True)
    return xc * lax.rsqrt(var + NORM_EPS) * g + b


def _post_kernel(y_ref, wo_ref, h_ref, g_ref, b_ref, wr_ref, br_ref,
                 h1_ref, h1b_ref, ri_ref, rw_ref, cnt_ref, carry_ref, *, alpha):
    i = pl.program_id(0)
    tm = h_ref.shape[0]

    @pl.when(i == 0)
    def _():
        carry_ref[...] = jnp.zeros_like(carry_ref)

    mix = jnp.dot(y_ref[...], wo_ref[...], preferred_element_type=F32)
    h1 = _layer_norm(alpha * h_ref[...] + mix, g_ref[...], b_ref[...])
    h1_ref[...] = h1
    h_hi = h1.astype(BF16)
    h1b_ref[...] = h_hi
    h_lo = (h1 - h_hi.astype(F32)).astype(BF16)
    wr = wr_ref[...]
    part = jnp.dot(h_hi, wr, preferred_element_type=F32)
    logits = (part[:, :LANES] + part[:, LANES:]
              + jnp.dot(h_lo, wr[:, :LANES], preferred_element_type=F32) + br_ref[...])
    lane = _iota((tm, LANES), 1)
    neg = -jnp.inf
    is_g = lane < MOE_GROUPS
    gl = jnp.where(is_g, logits, neg)
    gmax = jnp.max(gl, axis=-1, keepdims=True)
    gsel = jnp.min(jnp.where(gl == gmax, lane, LANES), axis=-1, keepdims=True)
    p_g = 1.0 / jnp.sum(jnp.where(is_g, jnp.exp(gl - gmax), 0.0), axis=-1, keepdims=True)
    eid = lane - MOE_GROUPS
    in_g = (eid >= 0) & (eid < MOE_EXPERTS) & ((eid // MOE_EXP_PER_GROUP) == gsel)
    el = jnp.where(in_g, logits, neg)
    v1 = jnp.max(el, axis=-1, keepdims=True)
    i1 = jnp.min(jnp.where(el == v1, lane, LANES), axis=-1, keepdims=True)
    el2 = jnp.where(lane == i1, neg, el)
    v2 = jnp.max(el2, axis=-1, keepdims=True)
    i2 = jnp.min(jnp.where(el2 == v2, lane, LANES), axis=-1, keepdims=True)
    t = jnp.exp(v2 - v1)
    w1 = p_g / (1.0 + t)
    w2 = p_g * t / (1.0 + t)
    e1 = i1 - MOE_GROUPS
    e2 = i2 - MOE_GROUPS
    oh = jnp.where(lane == e1, 1.0, 0.0) + jnp.where(lane == e2, 1.0, 0.0)
    strict = (_iota((tm, tm), 1) < _iota((tm, tm), 0)).astype(BF16)
    before = jnp.dot(strict, oh.astype(BF16), preferred_element_type=F32) + carry_ref[...]
    r1 = jnp.sum(jnp.where(lane == e1, before, 0.0), axis=-1, keepdims=True)
    r2 = jnp.sum(jnp.where(lane == e2, before, 0.0), axis=-1, keepdims=True)
    carry_ref[...] = carry_ref[...] + jnp.sum(oh, axis=0, keepdims=True)
    cnt_ref[...] = carry_ref[...]
    ri_ref[...] = jnp.where(lane == 0, e1, jnp.where(lane == 1, e2, jnp.where(
        lane == 2, r1.astype(I32), jnp.where(lane == 3, r2.astype(I32), 0))))
    rw_ref[...] = jnp.where(lane == 0, w1, jnp.where(lane == 1, w2, 0.0))


def _post_mixer(y, wo, h, g, b, wr, br, *, alpha):
    rows, kin = y.shape
    grid = (rows // POST_TILE,)
    rmap = lambda i: (i, 0)
    const = lambda i: (0, 0)
    in_specs = [
        pl.BlockSpec((POST_TILE, kin), rmap),
        pl.BlockSpec((kin, D_MODEL), const),
        pl.BlockSpec((POST_TILE, D_MODEL), rmap),
        pl.BlockSpec((1, D_MODEL), const),
        pl.BlockSpec((1, D_MODEL), const),
        pl.BlockSpec((D_MODEL, 2 * LANES), const),
        pl.BlockSpec((1, LANES), const),
    ]
    out_specs = [
        pl.BlockSpec((POST_TILE, D_MODEL), rmap),
        pl.BlockSpec((POST_TILE, D_MODEL), rmap),
        pl.BlockSpec((POST_TILE, LANES), rmap),
        pl.BlockSpec((POST_TILE, LANES), rmap),
        pl.BlockSpec((1, LANES), const),
    ]
    out_shape = [
        jax.ShapeDtypeStruct((rows, D_MODEL), F32),
        jax.ShapeDtypeStruct((rows, D_MODEL), BF16),
        jax.ShapeDtypeStruct((rows, LANES), I32),
        jax.ShapeDtypeStruct((rows, LANES), F32),
        jax.ShapeDtypeStruct((1, LANES), F32),
    ]
    return pl.pallas_call(
        functools.partial(_post_kernel, alpha=alpha), grid=grid, in_specs=in_specs, out_specs=out_specs,
        out_shape=out_shape, scratch_shapes=[pltpu.VMEM((1, LANES), F32)],
        compiler_params=_params(("arbitrary",)))(y, wo, h, g, b, wr, br)


def _expert_kernel(be_ref, nu_ref, x_ref, wg_ref, wu_ref, wd_ref, o_ref, wg_b, wu_b, wd_b):
    i = pl.program_id(0)
    prev = be_ref[jnp.maximum(i - 1, 0)]

    @pl.when(jnp.logical_or(i == 0, be_ref[i] != prev))
    def _():
        wg_b[...] = wg_ref[...].astype(BF16)
        wu_b[...] = wu_ref[...].astype(BF16)
        wd_b[...] = wd_ref[...].astype(BF16)

    @pl.when(i < nu_ref[0])
    def _():
        x = x_ref[...]
        hg = jnp.dot(x, wg_b[...], preferred_element_type=F32)
        hu = jnp.dot(x, wu_b[...], preferred_element_type=F32)
        hh = (hg * _sigmoid(hg) * hu).astype(BF16)
        o_ref[...] = jnp.dot(hh, wd_b[...], preferred_element_type=F32).astype(o_ref.dtype)

    @pl.when(i >= nu_ref[0])
    def _():
        o_ref[...] = jnp.zeros_like(o_ref)


def _experts(block_e, n_used, xb, w_gate, w_up, w_down, layer):
    nblk = xb.shape[0] // EXPERT_BLOCK
    wmap = lambda i, be, nu: (layer, be[i], 0, 0)
    grid_spec = pltpu.PrefetchScalarGridSpec(
        num_scalar_prefetch=2, grid=(nblk,),
        in_specs=[
            pl.BlockSpec((EXPERT_BLOCK, D_MODEL), lambda i, be, nu: (i, 0)),
            pl.BlockSpec((None, None, D_MODEL, EXPERT_FF), wmap),
            pl.BlockSpec((None, None, D_MODEL, EXPERT_FF), wmap),
            pl.BlockSpec((None, None, EXPERT_FF, D_MODEL), wmap),
        ],
        out_specs=pl.BlockSpec((EXPERT_BLOCK, D_MODEL), lambda i, be, nu: (i, 0)),
        scratch_shapes=[pltpu.VMEM((D_MODEL, EXPERT_FF), BF16), pltpu.VMEM((D_MODEL, EXPERT_FF), BF16),
                        pltpu.VMEM((EXPERT_FF, D_MODEL), BF16)])
    return pl.pallas_call(
        _expert_kernel, grid_spec=grid_spec,
        out_shape=jax.ShapeDtypeStruct((nblk * EXPERT_BLOCK, D_MODEL), BF16),
        compiler_params=_params(("arbitrary",)))(block_e, n_used, xb, w_gate, w_up, w_down)


def _combine_kernel(h_ref, ya_ref, yb_ref, rw_ref, g_ref, b_ref, o_ref, *, alpha):
    rw = rw_ref[...]
    ffn = ya_ref[...].astype(F32) * rw[:, 0:1] + yb_ref[...].astype(F32) * rw[:, 1:2]
    o_ref[...] = _layer_norm(alpha * h_ref[...] + ffn, g_ref[...], b_ref[...])


def _combine(h1, ya, yb, rw, g, b, *, alpha):
    rows = h1.shape[0]
    rmap = lambda i: (i, 0)
    const = lambda i: (0, 0)
    in_specs = [
        pl.BlockSpec((POST_TILE, D_MODEL), rmap), pl.BlockSpec((POST_TILE, D_MODEL), rmap),
        pl.BlockSpec((POST_TILE, D_MODEL), rmap), pl.BlockSpec((POST_TILE, LANES), rmap),
        pl.BlockSpec((1, D_MODEL), const), pl.BlockSpec((1, D_MODEL), const),
    ]
    return pl.pallas_call(
        functools.partial(_combine_kernel, alpha=alpha), grid=(rows // POST_TILE,), in_specs=in_specs,
        out_specs=pl.BlockSpec((POST_TILE, D_MODEL), rmap),
        out_shape=jax.ShapeDtypeStruct((rows, D_MODEL), F32),
        compiler_params=_params(("parallel",)))(h1, ya, yb, rw, g, b)


def _round_up(x, m):
    return (x + m - 1) // m * m


def _pad_cols(w, n):
    return jnp.pad(w, ((0, 0), (0, n - w.shape[1])))


def _take_rows(x, idx):
    return x.at[idx].get(mode='promise_in_bounds')


def _moe(h1b, ri, cnt, w_gate, w_up, w_down, layer):
    rows = h1b.shape[0]
    nblk = (2 * rows + MOE_EXPERTS * (EXPERT_BLOCK - 1) + EXPERT_BLOCK - 1) // EXPERT_BLOCK
    counts = cnt[0, :MOE_EXPERTS].astype(I32)
    pcounts = (counts + EXPERT_BLOCK - 1) // EXPERT_BLOCK * EXPERT_BLOCK
    pends = jnp.cumsum(pcounts)
    pstarts = pends - pcounts
    dest = _take_rows(pstarts, ri[:, 0:2].reshape(-1)).reshape(rows, 2) + ri[:, 2:4]
    tok = jnp.broadcast_to(jnp.arange(rows, dtype=I32)[:, None], (rows, 2))
    row_tok = jnp.zeros((nblk * EXPERT_BLOCK,), I32).at[dest.reshape(-1)].set(
        tok.reshape(-1), mode='promise_in_bounds', unique_indices=True)
    blk_start = jnp.arange(nblk, dtype=I32) * EXPERT_BLOCK
    block_e = jnp.minimum(jnp.sum((pends[None, :] <= blk_start[:, None]).astype(I32), axis=1), MOE_EXPERTS - 1)
    n_used = (pends[-1:] // EXPERT_BLOCK).astype(I32)
    xb = _take_rows(h1b, row_tok)
    yb = _experts(block_e, n_used, xb, w_gate, w_up, w_down, layer)
    return _take_rows(yb, dest[:, 0]), _take_rows(yb, dest[:, 1])


def kernel(x_prompt, x_sample, state_ssd, state_ssd_conv, state_gla, meta_tokens, ssd_w_in, ssd_conv_w, ssd_conv_b, ssd_dt_bias, ssd_A_log, ssd_D, ssd_norm_w, ssd_w_out, gla_w_in, gla_w_gk2, gla_b_gk2, gla_norm_w, gla_w_out, ln1_g, ln1_b, moe_w_grp, moe_b_grp, moe_w_exp, moe_b_exp, moe_w_gate, moe_w_up, moe_w_down, ln2_g, ln2_b):
    bp, seq, _ = x_prompt.shape
    bs, lsm, _ = x_sample.shape
    depth = ln1_g.shape[0]
    alpha = (2.0 * depth) ** 0.25
    lp = N_META + seq
    cp = PROMPT_CHUNK
    lp_pad = _round_up(lp, cp)
    cs = lsm
    off_s = bp * lp_pad
    assert off_s % cs == 0 and cs % GLA_SUB == 0 and cp % GLA_SUB == 0
    rows = _round_up(off_s + bs * lsm, POST_TILE)

    meta = jnp.broadcast_to(meta_tokens.astype(F32)[None], (bp, N_META, D_MODEL))
    hp = jnp.concatenate([meta, x_prompt, jnp.zeros((bp, lp_pad - lp, D_MODEL), F32)], axis=1)
    h = jnp.concatenate([hp.reshape(bp * lp_pad, D_MODEL), x_sample.reshape(bs * lsm, D_MODEL),
                         jnp.zeros((rows - off_s - bs * lsm, D_MODEL), F32)], axis=0)

    groups = (dict(row_off=0, nseq=bp, L=lp, Lpad=lp_pad, C=cp),
              dict(row_off=off_s, nseq=bs, L=lsm, Lpad=lsm, C=cs))

    def scatter_rows(parts):
        tail = rows - off_s - bs * lsm
        pieces = list(parts)
        if tail:
            pieces.append(jnp.zeros((tail, parts[0].shape[1]), parts[0].dtype))
        return jnp.concatenate(pieces, axis=0)

    new_ssd_p, new_ssd_s, new_conv_p, new_conv_s, new_gla_p, new_gla_s = [], [], [], [], [], []
    for i in range(depth):
        j = i // 2
        if i % 2 == 0:
            w_in = ssd_w_in[j].astype(BF16)
            z, xbc, dtr = _proj(
                h, [w_in[:, :SSD_INNER], w_in[:, SSD_INNER:SSD_INNER + SSD_CONV_DIM],
                    _pad_cols(w_in[:, SSD_INNER + SSD_CONV_DIM:], LANES)], [F32, F32, F32])
            cw = ssd_conv_w[j]
            cb = ssd_conv_b[j][None]
            dtb = _pad_cols(ssd_dt_bias[j][None], LANES)
            an = _pad_cols(-jnp.exp(ssd_A_log[j].astype(F32))[None], LANES)
            dsk = jnp.repeat(ssd_D[j], SSD_HEAD_DIM)[None]
            nw = ssd_norm_w[j][None]
            s0s = [jnp.zeros((bp, SSD_STATE, SSD_INNER), F32),
                   jnp.swapaxes(state_ssd[j].reshape(bs, SSD_INNER, SSD_STATE), 1, 2)]
            c0s = [jnp.zeros((bp, 8, SSD_CONV_DIM), F32),
                   jnp.pad(state_ssd_conv[j], ((0, 0), (8 - (SSD_CONV_W - 1), 0), (0, 0)))]
            ys = []
            for grp, s0, c0, acc_s, acc_c in zip(groups, s0s, c0s, (new_ssd_p, new_ssd_s), (new_conv_p, new_conv_s)):
                y, st = _ssd_scan(z, xbc, dtr, cw, cb, dtb, an, dsk, nw, s0, c0, **grp)
                ys.append(y)
                n, L, Lpad = grp['nseq'], grp['L'], grp['Lpad']
                acc_s.append(jnp.swapaxes(st, 1, 2).reshape(n, SSD_HEADS, SSD_HEAD_DIM, SSD_STATE))
                nconv = SSD_CONV_W - 1
                last = (grp['row_off'] + jnp.arange(n, dtype=I32)[:, None] * Lpad + (L - nconv)
                        + jnp.arange(nconv, dtype=I32)[None, :])
                acc_c.append(_take_rows(xbc, last.reshape(-1)).reshape(n, nconv, SSD_CONV_DIM))
            ymix = scatter_rows(ys)
            wo = ssd_w_out[j].astype(BF16)
        else:
            w_in = gla_w_in[j].astype(BF16)
            nq = 2 * GLA_KEY_DIM + 2 * GLA_VAL_DIM
            qkvg, lr = _proj(h, [w_in[:, :nq], _pad_cols(w_in[:, nq:], LANES)], [F32, F32])
            w2 = jnp.pad(gla_w_gk2[j], ((0, LANES - GLA_GATE_RANK), (0, 0))).astype(BF16)
            b2 = gla_b_gk2[j][None]
            nw = gla_norm_w[j][None]
            s0s = [jnp.zeros((bp, GLA_VAL_DIM, GLA_DK), F32),
                   jnp.swapaxes(state_gla[j], 2, 3).reshape(bs, GLA_VAL_DIM, GLA_DK)]
            ys = []
            for grp, s0, acc in zip(groups, s0s, (new_gla_p, new_gla_s)):
                y, st = _gla_scan(qkvg, lr, w2, b2, nw, s0, **grp)
                ys.append(y)
                acc.append(jnp.swapaxes(st.reshape(grp['nseq'], GLA_HEADS, GLA_DV, GLA_DK), 2, 3))
            ymix = scatter_rows(ys)
            wo = gla_w_out[j].astype(BF16)

        wr = _pad_cols(jnp.concatenate([moe_w_grp[i], moe_w_exp[i]], axis=1), LANES)
        br = _pad_cols(jnp.concatenate([moe_b_grp[i], moe_b_exp[i]])[None], LANES)
        wr_hi = wr.astype(BF16)
        wr = jnp.concatenate([wr_hi, (wr - wr_hi.astype(F32)).astype(BF16)], axis=1)
        h1, h1b, ri, rw, cnt = _post_mixer(ymix, wo, h, ln1_g[i][None], ln1_b[i][None], wr, br, alpha=alpha)
        ya, yb = _moe(h1b, ri, cnt, moe_w_gate, moe_w_up, moe_w_down, i)
        h = _combine(h1, ya, yb, rw, ln2_g[i][None], ln2_b[i][None], alpha=alpha)

    y_prompt = h[:off_s].reshape(bp, lp_pad, D_MODEL)[:, N_META:lp]
    y_sample = h[off_s:off_s + bs * lsm].reshape(bs, lsm, D_MODEL)
    return (y_prompt, y_sample, jnp.stack(new_ssd_p), jnp.stack(new_conv_p), jnp.stack(new_gla_p),
            jnp.stack(new_ssd_s), jnp.stack(new_conv_s), jnp.stack(new_gla_s))
```

```python
import functools
import math

import jax
import jax.numpy as jnp
from jax import lax
from jax.experimental import pallas as pl
from jax.experimental.pallas import tpu as pltpu

F32 = jnp.float32
BF16 = jnp.bfloat16
I32 = jnp.int32
HI = lax.Precision.HIGHEST
NT = (((1,), (1,)), ((), ()))
TN = (((0,), (0,)), ((), ()))

D_MODEL = 1024
N_META = 16
SSD_INNER = 2048
SSD_HEAD_DIM = 64
SSD_HEADS = 32
SSD_GROUPS = 4
SSD_STATE = 128
SSD_CONV_W = 4
SSD_CONV_DIM = 3072
GLA_HEADS = 4
GLA_KEY_DIM = 512
GLA_VAL_DIM = 1024
GLA_DK = 128
GLA_DV = 256
GLA_GATE_RANK = 16
GLA_GATE_NORM = 16.0
MOE_GROUPS = 4
MOE_EXP_PER_GROUP = 8
MOE_EXPERTS = 32
EXPERT_FF = 512
NORM_EPS = 1e-5

LANES = 128
VMEM_LIMIT_MB = 56
ROW_TILE = 256
POST_TILE = 512
PROMPT_CHUNK = 128
GLA_SUB = 16
EXPERT_BLOCK = 256


def _params(sem):
    return pltpu.CompilerParams(dimension_semantics=sem, vmem_limit_bytes=VMEM_LIMIT_MB << 20)


def _sigmoid(x):
    return 1.0 / (1.0 + jnp.exp(-x))


def _softplus(x):
    return jnp.maximum(x, 0.0) + jnp.log1p(jnp.exp(-jnp.abs(x)))


def _iota(shape, dim):
    return lax.broadcasted_iota(I32, shape, dim)


def _proj_kernel(x_ref, *refs):
    n = len(refs) // 2
    xb = x_ref[...].astype(BF16)
    for w_ref, o_ref in zip(refs[:n], refs[n:]):
        o_ref[...] = jnp.dot(xb, w_ref[...], preferred_element_type=F32).astype(o_ref.dtype)


def _proj(x, ws, out_dtypes):
    rows, k = x.shape
    grid = (rows // ROW_TILE,)
    in_specs = [pl.BlockSpec((ROW_TILE, k), lambda i: (i, 0))]
    in_specs += [pl.BlockSpec(w.shape, lambda i: (0, 0)) for w in ws]
    out_specs = [pl.BlockSpec((ROW_TILE, w.shape[1]), lambda i: (i, 0)) for w in ws]
    out_shape = [jax.ShapeDtypeStruct((rows, w.shape[1]), dt) for w, dt in zip(ws, out_dtypes)]
    return pl.pallas_call(
        _proj_kernel, grid=grid, in_specs=in_specs, out_specs=out_specs, out_shape=out_shape,
        compiler_params=_params(("parallel",)))(x, *ws)


def _ssd_kernel(z_ref, xbc_ref, dt_ref, cw_ref, cb_ref, dtb_ref, an_ref, dsk_ref, nw_ref, s0_ref, c0_ref,
                yprev_ref, y_ref, sout_ref, st_ref, xc_ref, u_ref, *, C, L):
    del yprev_ref
    c = pl.program_id(1)
    hist = 8

    @pl.when(c == 0)
    def _():
        st_ref[...] = s0_ref[0]
        xc_ref[0:hist, :] = c0_ref[0]

    @pl.when(c > 0)
    def _():
        xc_ref[0:hist, :] = xc_ref[C:C + hist, :]

    xc_ref[hist:hist + C, :] = xbc_ref[...].astype(F32)

    for j in range(0, SSD_CONV_DIM, 512):
        sl = slice(j, j + 512)
        acc = cb_ref[:, sl] + xc_ref[hist:hist + C, sl] * cw_ref[3:4, sl]
        for w in range(SSD_CONV_W - 1):
            off = hist - (SSD_CONV_W - 1) + w
            acc = acc + xc_ref[off:off + C, sl] * cw_ref[w:w + 1, sl]
        u_ref[:, sl] = acc * _sigmoid(acc)

    row = _iota((C, LANES), 0) + c * C
    dt = jnp.where(row < L, _softplus(dt_ref[...] + dtb_ref[...]), 0.0)
    loga = dt * an_ref[...]
    ti = _iota((C, C), 0)
    si = _iota((C, C), 1)
    tril = si <= ti
    cum = jnp.dot(tril.astype(F32), loga, precision=HI, preferred_element_type=F32)
    eye = (_iota((LANES, LANES), 0) == _iota((LANES, LANES), 1)).astype(F32)
    cum_t = lax.dot_general(eye, cum, NT, precision=HI, preferred_element_type=F32)
    dt_t = lax.dot_general(eye, dt, NT, precision=HI, preferred_element_type=F32)
    w_t = dt_t * jnp.exp(cum_t[:, C - 1:C] - cum_t)
    e_last = jnp.exp(cum[C - 1:C, :])
    eye_b = eye.astype(BF16)
    lane = _iota((C, LANES), 1)
    lane_s = _iota((SSD_STATE, LANES), 1)
    lane_r = _iota((1, LANES), 1)
    hpg = SSD_HEADS // SSD_GROUPS

    for g in range(SSD_GROUPS):
        b_f = u_ref[:, SSD_INNER + g * SSD_STATE:SSD_INNER + (g + 1) * SSD_STATE]
        c_f = u_ref[:, SSD_INNER + (SSD_GROUPS + g) * SSD_STATE:SSD_INNER + (SSD_GROUPS + g + 1) * SSD_STATE]
        b_b = b_f.astype(BF16)
        cb = lax.dot_general(c_f.astype(BF16), b_b, NT, preferred_element_type=F32)
        b_t = lax.dot_general(eye_b, b_b, NT, preferred_element_type=F32)
        ypairs = []
        for jp in range(hpg // 2):
            p = g * (hpg // 2) + jp
            cs = slice(p * LANES, (p + 1) * LANES)
            xs_p = u_ref[:, cs]
            xs_b = xs_p.astype(BF16)
            st_p = st_ref[:, cs]
            st_b = st_p.astype(BF16)
            res, upd = [], []
            for h in (2 * p, 2 * p + 1):
                ccol = cum[:, h:h + 1]
                dec = jnp.where(tril, jnp.exp(jnp.minimum(ccol - cum_t[h:h + 1, :], 0.0)), 0.0)
                wmat = (cb * dec * dt_t[h:h + 1, :]).astype(BF16)
                c_e = (c_f * jnp.exp(ccol)).astype(BF16)
                res.append(jnp.dot(wmat, xs_b, preferred_element_type=F32)
                           + jnp.dot(c_e, st_b, preferred_element_type=F32))
                b_w = (b_t * w_t[h:h + 1, :]).astype(BF16)
                upd.append(jnp.dot(b_w, xs_b, preferred_element_type=F32))
            y_p = jnp.where(lane < SSD_HEAD_DIM, res[0], res[1]) + dsk_ref[:, cs] * xs_p
            e_p = jnp.where(lane_r < SSD_HEAD_DIM, e_last[:, 2 * p:2 * p + 1], e_last[:, 2 * p + 1:2 * p + 2])
            st_ref[:, cs] = st_p * e_p + jnp.where(lane_s < SSD_HEAD_DIM, upd[0], upd[1])
            ypairs.append(y_p)
        gs = slice(g * 512, (g + 1) * 512)
        yg = jnp.concatenate(ypairs, axis=1)
        zg = z_ref[:, gs].astype(F32)
        yg = yg * (zg * _sigmoid(zg))
        ms = jnp.mean(yg * yg, axis=-1, keepdims=True)
        y_ref[:, gs] = (yg * lax.rsqrt(ms + NORM_EPS) * nw_ref[:, gs]).astype(y_ref.dtype)

    @pl.when(c == pl.num_programs(1) - 1)
    def _():
        sout_ref[0] = st_ref[...]


def _ssd_scan(z, xbc, dt, cw, cb, dtb, an, dsk, nw, s0_t, c0, y_prev, *, row_off, nseq, nwalk, L, Lpad, C):
    nc = Lpad // C
    base = row_off // C

    def rmap(b, c):
        return (base + b * nc + c, 0)

    const = lambda b, c: (0, 0)
    in_specs = [
        pl.BlockSpec((C, SSD_INNER), rmap),
        pl.BlockSpec((C, SSD_CONV_DIM), rmap),
        pl.BlockSpec((C, LANES), rmap),
        pl.BlockSpec((SSD_CONV_W, SSD_CONV_DIM), const),
        pl.BlockSpec((1, SSD_CONV_DIM), const),
        pl.BlockSpec((1, LANES), const),
        pl.BlockSpec((1, LANES), const),
        pl.BlockSpec((1, SSD_INNER), const),
        pl.BlockSpec((1, SSD_INNER), const),
        pl.BlockSpec((1, SSD_STATE, SSD_INNER), lambda b, c: (jnp.minimum(b, nseq - 1), 0, 0)),
        pl.BlockSpec((1, 8, SSD_CONV_DIM), lambda b, c: (jnp.minimum(b, nseq - 1), 0, 0)),
        pl.BlockSpec(memory_space=pl.ANY),
    ]
    out_specs = [
        pl.BlockSpec((C, SSD_INNER), rmap),
        pl.BlockSpec((1, SSD_STATE, SSD_INNER), lambda b, c: (b, 0, 0)),
    ]
    out_shape = [
        jax.ShapeDtypeStruct(y_prev.shape, y_prev.dtype),
        jax.ShapeDtypeStruct((nwalk, SSD_STATE, SSD_INNER), F32),
    ]
    scratch = [
        pltpu.VMEM((SSD_STATE, SSD_INNER), F32),
        pltpu.VMEM((C + 8, SSD_CONV_DIM), F32),
        pltpu.VMEM((C, SSD_CONV_DIM), F32),
    ]
    y, st = pl.pallas_call(
        functools.partial(_ssd_kernel, C=C, L=L), grid=(nwalk, nc), in_specs=in_specs, out_specs=out_specs,
        out_shape=out_shape, scratch_shapes=scratch, input_output_aliases={11: 0},
        compiler_params=_params(("arbitrary", "arbitrary")))(z, xbc, dt, cw, cb, dtb, an, dsk, nw, s0_t, c0, y_prev)
    return y, st[:nseq]


def _gla_kernel(q_ref, k_ref, v_ref, g_ref, lr_ref, w2_ref, b2_ref, nw_ref, s0_ref,
                oprev_ref, o_ref, sout_ref, st_ref, *, C, L):
    del oprev_ref
    c = pl.program_id(1)
    sub = GLA_SUB
    nb = C // sub

    @pl.when(c == 0)
    def _():
        st_ref[...] = s0_ref[0]

    row_k = _iota((C, GLA_KEY_DIM), 0) + c * C
    gk = jnp.dot(lr_ref[...].astype(BF16), w2_ref[...], preferred_element_type=F32) + b2_ref[...]
    logg = (jnp.minimum(gk, 0.0) - jnp.log1p(jnp.exp(-jnp.abs(gk)))) * (1.0 / GLA_GATE_NORM)
    logg = jnp.where(row_k < L, logg, 0.0)
    ti = _iota((C, C), 0)
    si = _iota((C, C), 1)
    tril = si <= ti
    cum_all = jnp.dot(tril.astype(F32), logg, precision=HI, preferred_element_type=F32)
    same_sub = (ti // sub) == (si // sub)
    diag_mask = jnp.logical_and(tril, same_sub)
    sel = ((_iota((sub * GLA_DK, C), 0) // GLA_DK) == (_iota((sub * GLA_DK, C), 1) % sub)).astype(BF16)
    row_v = _iota((C, GLA_DV), 0) + c * C
    valid_v = row_v < L
    scale = GLA_DK ** -0.5

    for h in range(GLA_HEADS):
        ks = slice(h * GLA_DK, (h + 1) * GLA_DK)
        vs = slice(h * GLA_DV, (h + 1) * GLA_DV)
        q = q_ref[:, ks].astype(F32) * scale
        k = k_ref[:, ks].astype(F32)
        v = jnp.where(valid_v, v_ref[:, vs].astype(F32), 0.0)
        v_b = v.astype(BF16)
        cum = cum_all[:, ks]
        st = st_ref[h * GLA_DV:(h + 1) * GLA_DV, :]
        q_e = (q * jnp.exp(cum)).astype(BF16)
        y = lax.dot_general(q_e, st.astype(BF16), NT, preferred_element_type=F32)
        if nb > 1:
            q_parts, k_parts = [], []
            for i in range(1, nb):
                r_i = cum[i * sub - 1:i * sub, :]
                qi = q[i * sub:(i + 1) * sub, :] * jnp.exp(cum[i * sub:(i + 1) * sub, :] - r_i)
                q_parts.append(jnp.concatenate(
                    [jnp.zeros((i * sub, GLA_DK), F32), qi, jnp.zeros((C - (i + 1) * sub, GLA_DK), F32)], axis=0)
                    if C - (i + 1) * sub > 0 else jnp.concatenate([jnp.zeros((i * sub, GLA_DK), F32), qi], axis=0))
                ki = k[0:i * sub, :] * jnp.exp(r_i - cum[0:i * sub, :])
                k_parts.append(jnp.concatenate([ki, jnp.zeros((C - i * sub, GLA_DK), F32)], axis=0))
            q_st = jnp.concatenate(q_parts, axis=1).astype(BF16)
            k_st = jnp.concatenate(k_parts, axis=1).astype(BF16)
            a_off = lax.dot_general(q_st, k_st, NT, preferred_element_type=F32)
        else:
            a_off = jnp.zeros((C, C), F32)
        q3 = q.reshape(nb, sub, GLA_DK)
        k3 = k.reshape(nb, sub, GLA_DK)
        c3 = cum.reshape(nb, sub, GLA_DK)
        p_parts = []
        for o in range(sub):
            p_o = q3 * jnp.exp(jnp.minimum(c3 - c3[:, o:o + 1, :], 0.0)) * k3[:, o:o + 1, :]
            p_parts.append(p_o.reshape(C, GLA_DK).astype(BF16))
        p_st = jnp.concatenate(p_parts, axis=1)
        a_diag = jnp.dot(p_st, sel, preferred_element_type=F32)
        a = a_off + jnp.where(diag_mask, a_diag, 0.0)
        y = y + jnp.dot(a.astype(BF16), v_b, preferred_element_type=F32)
        last = cum[C - 1:C, :]
        k_e = (k * jnp.exp(last - cum)).astype(BF16)
        st_ref[h * GLA_DV:(h + 1) * GLA_DV, :] = (
            st * jnp.exp(last) + lax.dot_general(v_b, k_e, TN, preferred_element_type=F32))
        ms = jnp.mean(y * y, axis=-1, keepdims=True)
        gate = g_ref[:, vs].astype(F32)
        o_ref[:, vs] = (y * lax.rsqrt(ms + NORM_EPS) * nw_ref[...] * (gate * _sigmoid(gate))).astype(o_ref.dtype)

    @pl.when(c == pl.num_programs(1) - 1)
    def _():
        sout_ref[0] = st_ref[...]


def _gla_scan(qkvg, lr, w2, b2, nw, s0_t, o_prev, *, row_off, nseq, nwalk, L, Lpad, C):
    nc = Lpad // C
    base = row_off // C
    const = lambda b, c: (0, 0)
    in_specs = [
        pl.BlockSpec((C, GLA_KEY_DIM), lambda b, c: (base + b * nc + c, 0)),
        pl.BlockSpec((C, GLA_KEY_DIM), lambda b, c: (base + b * nc + c, 1)),
        pl.BlockSpec((C, GLA_VAL_DIM), lambda b, c: (base + b * nc + c, 1)),
        pl.BlockSpec((C, GLA_VAL_DIM), lambda b, c: (base + b * nc + c, 2)),
        pl.BlockSpec((C, LANES), lambda b, c: (base + b * nc + c, 0)),
        pl.BlockSpec((LANES, GLA_KEY_DIM), const),
        pl.BlockSpec((1, GLA_KEY_DIM), const),
        pl.BlockSpec((1, GLA_DV), const),
        pl.BlockSpec((1, GLA_VAL_DIM, GLA_DK), lambda b, c: (jnp.minimum(b, nseq - 1), 0, 0)),
        pl.BlockSpec(memory_space=pl.ANY),
    ]
    out_specs = [
        pl.BlockSpec((C, GLA_VAL_DIM), lambda b, c: (base + b * nc + c, 0)),
        pl.BlockSpec((1, GLA_VAL_DIM, GLA_DK), lambda b, c: (b, 0, 0)),
    ]
    out_shape = [
        jax.ShapeDtypeStruct(o_prev.shape, o_prev.dtype),
        jax.ShapeDtypeStruct((nwalk, GLA_VAL_DIM, GLA_DK), F32),
    ]
    o, st = pl.pallas_call(
        functools.partial(_gla_kernel, C=C, L=L), grid=(nwalk, nc), in_specs=in_specs, out_specs=out_specs,
        out_shape=out_shape, scratch_shapes=[pltpu.VMEM((GLA_VAL_DIM, GLA_DK), F32)],
        input_output_aliases={9: 0},
        compiler_params=_params(("arbitrary", "arbitrary")))(qkvg, qkvg, qkvg, qkvg, lr, w2, b2, nw, s0_t, o_prev)
    return o, st[:nseq]


def _layer_norm(x, g, b):
    mu = jnp.mean(x, axis=-1, keepdims=True)
    xc = x - mu
    var = jnp.mean(xc * xc, axis=-1, keepdims=True)
    return xc * lax.rsqrt(var + NORM_EPS) * g + b


def _post_kernel(y_ref, wo_ref, h_ref, g_ref, b_ref, wr_ref, br_ref,
                 h1_ref, h1b_ref, ri_ref, rw_ref, cnt_ref, carry_ref, *, alpha):
    i = pl.program_id(0)
    tm = h_ref.shape[0]

    @pl.when(i == 0)
    def _():
        carry_ref[...] = jnp.zeros_like(carry_ref)

    mix = jnp.dot(y_ref[...], wo_ref[...], preferred_element_type=F32)
    h1 = _layer_norm(alpha * h_ref[...] + mix, g_ref[...], b_ref[...])
    h1_ref[...] = h1
    h_hi = h1.astype(BF16)
    h1b_ref[...] = h_hi
    logits = jnp.dot(h_hi, wr_ref[...], preferred_element_type=F32) + br_ref[...]
    lane = _iota((tm, LANES), 1)
    neg = -jnp.inf
    is_g = lane < MOE_GROUPS
    gl = jnp.where(is_g, logits, neg)
    gmax = jnp.max(gl, axis=-1, keepdims=True)
    gsel = jnp.min(jnp.where(gl == gmax, lane, LANES), axis=-1, keepdims=True)
    p_g = 1.0 / jnp.sum(jnp.where(is_g, jnp.exp(gl - gmax), 0.0), axis=-1, keepdims=True)
    eid = lane - MOE_GROUPS
    in_g = (eid >= 0) & (eid < MOE_EXPERTS) & ((eid // MOE_EXP_PER_GROUP) == gsel)
    el = jnp.where(in_g, logits, neg)
    v1 = jnp.max(el, axis=-1, keepdims=True)
    i1 = jnp.min(jnp.where(el == v1, lane, LANES), axis=-1, keepdims=True)
    el2 = jnp.where(lane == i1, neg, el)
    v2 = jnp.max(el2, axis=-1, keepdims=True)
    i2 = jnp.min(jnp.where(el2 == v2, lane, LANES), axis=-1, keepdims=True)
    t = jnp.exp(v2 - v1)
    w1 = p_g / (1.0 + t)
    w2 = p_g * t / (1.0 + t)
    e1 = i1 - MOE_GROUPS
    e2 = i2 - MOE_GROUPS
    oh = jnp.where(lane == e1, 1.0, 0.0) + jnp.where(lane == e2, 1.0, 0.0)
    strict = (_iota((tm, tm), 1) < _iota((tm, tm), 0)).astype(BF16)
    before = jnp.dot(strict, oh.astype(BF16), preferred_element_type=F32) + carry_ref[...]
    r1 = jnp.sum(jnp.where(lane == e1, before, 0.0), axis=-1, keepdims=True)
    r2 = jnp.sum(jnp.where(lane == e2, before, 0.0), axis=-1, keepdims=True)
    carry_ref[...] = carry_ref[...] + jnp.sum(oh, axis=0, keepdims=True)
    cnt_ref[...] = carry_ref[...]
    ri_ref[...] = jnp.where(lane == 0, e1, jnp.where(lane == 1, e2, jnp.where(
        lane == 2, r1.astype(I32), jnp.where(lane == 3, r2.astype(I32), 0))))
    rw_ref[...] = jnp.where(lane == 0, w1, jnp.where(lane == 1, w2, 0.0))


def _post_mixer(y, wo, h, g, b, wr, br, *, alpha):
    rows, kin = y.shape
    grid = (rows // POST_TILE,)
    rmap = lambda i: (i, 0)
    const = lambda i: (0, 0)
    in_specs = [
        pl.BlockSpec((POST_TILE, kin), rmap),
        pl.BlockSpec((kin, D_MODEL), const),
        pl.BlockSpec((POST_TILE, D_MODEL), rmap),
        pl.BlockSpec((1, D_MODEL), const),
        pl.BlockSpec((1, D_MODEL), const),
        pl.BlockSpec((D_MODEL, LANES), const),
        pl.BlockSpec((1, LANES), const),
    ]
    out_specs = [
        pl.BlockSpec((POST_TILE, D_MODEL), rmap),
        pl.BlockSpec((POST_TILE, D_MODEL), rmap),
        pl.BlockSpec((POST_TILE, LANES), rmap),
        pl.BlockSpec((POST_TILE, LANES), rmap),
        pl.BlockSpec((1, LANES), const),
    ]
    out_shape = [
        jax.ShapeDtypeStruct((rows, D_MODEL), F32),
        jax.ShapeDtypeStruct((rows, D_MODEL), BF16),
        jax.ShapeDtypeStruct((rows, LANES), I32),
        jax.ShapeDtypeStruct((rows, LANES), F32),
        jax.ShapeDtypeStruct((1, LANES), F32),
    ]
    return pl.pallas_call(
        functools.partial(_post_kernel, alpha=alpha), grid=grid, in_specs=in_specs, out_specs=out_specs,
        out_shape=out_shape, scratch_shapes=[pltpu.VMEM((1, LANES), F32)],
        compiler_params=_params(("arbitrary",)))(y, wo, h, g, b, wr, br)


def _expert_kernel(be_ref, nu_ref, x_ref, wg_ref, wu_ref, wd_ref, o_ref, wg_b, wu_b, wd_b):
    i = pl.program_id(0)
    prev = be_ref[jnp.maximum(i - 1, 0)]

    @pl.when(jnp.logical_or(i == 0, be_ref[i] != prev))
    def _():
        wg_b[...] = wg_ref[...].astype(BF16)
        wu_b[...] = wu_ref[...].astype(BF16)
        wd_b[...] = wd_ref[...].astype(BF16)

    @pl.when(i < nu_ref[0])
    def _():
        x = x_ref[...]
        hg = jnp.dot(x, wg_b[...], preferred_element_type=F32)
        hu = jnp.dot(x, wu_b[...], preferred_element_type=F32)
        hh = (hg * _sigmoid(hg) * hu).astype(BF16)
        o_ref[...] = jnp.dot(hh, wd_b[...], preferred_element_type=F32).astype(o_ref.dtype)

    @pl.when(i >= nu_ref[0])
    def _():
        o_ref[...] = jnp.zeros_like(o_ref)


def _experts(block_e, n_used, xb, w_gate, w_up, w_down, layer):
    nblk = xb.shape[0] // EXPERT_BLOCK
    wmap = lambda i, be, nu: (layer, be[i], 0, 0)
    grid_spec = pltpu.PrefetchScalarGridSpec(
        num_scalar_prefetch=2, grid=(nblk,),
        in_specs=[
            pl.BlockSpec((EXPERT_BLOCK, D_MODEL), lambda i, be, nu: (i, 0)),
            pl.BlockSpec((None, None, D_MODEL, EXPERT_FF), wmap),
            pl.BlockSpec((None, None, D_MODEL, EXPERT_FF), wmap),
            pl.BlockSpec((None, None, EXPERT_FF, D_MODEL), wmap),
        ],
        out_specs=pl.BlockSpec((EXPERT_BLOCK, D_MODEL), lambda i, be, nu: (i, 0)),
        scratch_shapes=[pltpu.VMEM((D_MODEL, EXPERT_FF), BF16), pltpu.VMEM((D_MODEL, EXPERT_FF), BF16),
                        pltpu.VMEM((EXPERT_FF, D_MODEL), BF16)])
    return pl.pallas_call(
        _expert_kernel, grid_spec=grid_spec,
        out_shape=jax.ShapeDtypeStruct((nblk * EXPERT_BLOCK, D_MODEL), BF16),
        compiler_params=_params(("arbitrary",)))(block_e, n_used, xb, w_gate, w_up, w_down)


def _combine_kernel(h_ref, ya_ref, yb_ref, rw_ref, g_ref, b_ref, o_ref, *, alpha):
    rw = rw_ref[...]
    ffn = ya_ref[...].astype(F32) * rw[:, 0:1] + yb_ref[...].astype(F32) * rw[:, 1:2]
    o_ref[...] = _layer_norm(alpha * h_ref[...] + ffn, g_ref[...], b_ref[...])


def _combine(h1, ya, yb, rw, g, b, *, alpha):
    rows = h1.shape[0]
    rmap = lambda i: (i, 0)
    const = lambda i: (0, 0)
    in_specs = [
        pl.BlockSpec((POST_TILE, D_MODEL), rmap), pl.BlockSpec((POST_TILE, D_MODEL), rmap),
        pl.BlockSpec((POST_TILE, D_MODEL), rmap), pl.BlockSpec((POST_TILE, LANES), rmap),
        pl.BlockSpec((1, D_MODEL), const), pl.BlockSpec((1, D_MODEL), const),
    ]
    return pl.pallas_call(
        functools.partial(_combine_kernel, alpha=alpha), grid=(rows // POST_TILE,), in_specs=in_specs,
        out_specs=pl.BlockSpec((POST_TILE, D_MODEL), rmap),
        out_shape=jax.ShapeDtypeStruct((rows, D_MODEL), F32),
        compiler_params=_params(("parallel",)))(h1, ya, yb, rw, g, b)


def _round_up(x, m):
    return (x + m - 1) // m * m


def _pad_cols(w, n):
    return jnp.pad(w, ((0, 0), (0, n - w.shape[1])))


def _take_rows(x, idx):
    return x.at[idx].get(mode='promise_in_bounds')


def _moe(h1b, ri, cnt, w_gate, w_up, w_down, layer):
    rows = h1b.shape[0]
    nblk = (2 * rows + MOE_EXPERTS * (EXPERT_BLOCK - 1) + EXPERT_BLOCK - 1) // EXPERT_BLOCK
    counts = cnt[0, :MOE_EXPERTS].astype(I32)
    pcounts = (counts + EXPERT_BLOCK - 1) // EXPERT_BLOCK * EXPERT_BLOCK
    pends = jnp.cumsum(pcounts)
    pstarts = pends - pcounts
    dest = _take_rows(pstarts, ri[:, 0:2].reshape(-1)).reshape(rows, 2) + ri[:, 2:4]
    tok = jnp.broadcast_to(jnp.arange(rows, dtype=I32)[:, None], (rows, 2))
    row_tok = (jnp.arange(nblk * EXPERT_BLOCK, dtype=I32) % rows).at[dest.reshape(-1)].set(
        tok.reshape(-1), mode='promise_in_bounds', unique_indices=True)
    blk_start = jnp.arange(nblk, dtype=I32) * EXPERT_BLOCK
    block_e = jnp.minimum(jnp.sum((pends[None, :] <= blk_start[:, None]).astype(I32), axis=1), MOE_EXPERTS - 1)
    n_used = (pends[-1:] // EXPERT_BLOCK).astype(I32)
    xb = _take_rows(h1b, row_tok)
    yb = _experts(block_e, n_used, xb, w_gate, w_up, w_down, layer)
    return _take_rows(yb, dest[:, 0]), _take_rows(yb, dest[:, 1])


def kernel(x_prompt, x_sample, state_ssd, state_ssd_conv, state_gla, meta_tokens, ssd_w_in, ssd_conv_w, ssd_conv_b, ssd_dt_bias, ssd_A_log, ssd_D, ssd_norm_w, ssd_w_out, gla_w_in, gla_w_gk2, gla_b_gk2, gla_norm_w, gla_w_out, ln1_g, ln1_b, moe_w_grp, moe_b_grp, moe_w_exp, moe_b_exp, moe_w_gate, moe_w_up, moe_w_down, ln2_g, ln2_b):
    bp, seq, _ = x_prompt.shape
    bs, lsm, _ = x_sample.shape
    depth = ln1_g.shape[0]
    alpha = (2.0 * depth) ** 0.25
    lp = N_META + seq
    cp = PROMPT_CHUNK
    lp_pad = _round_up(lp, cp)
    cs = lsm
    off_s = bp * lp_pad
    assert off_s % cs == 0 and cs % GLA_SUB == 0 and cp % GLA_SUB == 0
    rows = _round_up(off_s + bs * lsm, POST_TILE)

    pieces = []
    for b in range(bp):
        pieces += [meta_tokens.astype(F32), x_prompt[b], jnp.zeros((lp_pad - lp, D_MODEL), F32)]
    pieces += [x_sample.reshape(bs * lsm, D_MODEL), jnp.zeros((rows - off_s - bs * lsm, D_MODEL), F32)]
    h = jnp.concatenate(pieces, axis=0)

    groups = (dict(row_off=0, nseq=bp, nwalk=bp, L=lp, Lpad=lp_pad, C=cp),
              dict(row_off=off_s, nseq=bs, nwalk=(rows - off_s) // cs, L=lsm, Lpad=lsm, C=cs))
    ymix_ssd = jnp.zeros((rows, SSD_INNER), BF16)
    ymix_gla = jnp.zeros((rows, GLA_VAL_DIM), BF16)

    new_ssd_p, new_ssd_s, new_conv_p, new_conv_s, new_gla_p, new_gla_s = [], [], [], [], [], []
    for i in range(depth):
        j = i // 2
        if i % 2 == 0:
            w_in = ssd_w_in[j].astype(BF16)
            z, xbc, dtr = _proj(
                h, [w_in[:, :SSD_INNER], w_in[:, SSD_INNER:SSD_INNER + SSD_CONV_DIM],
                    _pad_cols(w_in[:, SSD_INNER + SSD_CONV_DIM:], LANES)], [F32, F32, F32])
            cw = ssd_conv_w[j]
            cb = ssd_conv_b[j][None]
            dtb = _pad_cols(ssd_dt_bias[j][None], LANES)
            an = _pad_cols(-jnp.exp(ssd_A_log[j].astype(F32))[None], LANES)
            dsk = jnp.repeat(ssd_D[j], SSD_HEAD_DIM)[None]
            nw = ssd_norm_w[j][None]
            s0s = [jnp.zeros((bp, SSD_STATE, SSD_INNER), F32),
                   jnp.swapaxes(state_ssd[j].reshape(bs, SSD_INNER, SSD_STATE), 1, 2)]
            c0s = [jnp.zeros((bp, 8, SSD_CONV_DIM), F32),
                   jnp.pad(state_ssd_conv[j], ((0, 0), (8 - (SSD_CONV_W - 1), 0), (0, 0)))]
            for grp, s0, c0, acc_s, acc_c in zip(groups, s0s, c0s, (new_ssd_p, new_ssd_s), (new_conv_p, new_conv_s)):
                ymix_ssd, st = _ssd_scan(z, xbc, dtr, cw, cb, dtb, an, dsk, nw, s0, c0, ymix_ssd, **grp)
                n, L, Lpad = grp['nseq'], grp['L'], grp['Lpad']
                acc_s.append(jnp.swapaxes(st, 1, 2).reshape(n, SSD_HEADS, SSD_HEAD_DIM, SSD_STATE))
                nconv = SSD_CONV_W - 1
                last = (grp['row_off'] + jnp.arange(n, dtype=I32)[:, None] * Lpad + (L - nconv)
                        + jnp.arange(nconv, dtype=I32)[None, :])
                acc_c.append(_take_rows(xbc, last.reshape(-1)).reshape(n, nconv, SSD_CONV_DIM))
            ymix = ymix_ssd
            wo = ssd_w_out[j].astype(BF16)
        else:
            w_in = gla_w_in[j].astype(BF16)
            nq = 2 * GLA_KEY_DIM + 2 * GLA_VAL_DIM
            qkvg, lr = _proj(h, [w_in[:, :nq], _pad_cols(w_in[:, nq:], LANES)], [F32, F32])
            w2 = jnp.pad(gla_w_gk2[j], ((0, LANES - GLA_GATE_RANK), (0, 0))).astype(BF16)
            b2 = gla_b_gk2[j][None]
            nw = gla_norm_w[j][None]
            s0s = [jnp.zeros((bp, GLA_VAL_DIM, GLA_DK), F32),
                   jnp.swapaxes(state_gla[j], 2, 3).reshape(bs, GLA_VAL_DIM, GLA_DK)]
            for grp, s0, acc in zip(groups, s0s, (new_gla_p, new_gla_s)):
                ymix_gla, st = _gla_scan(qkvg, lr, w2, b2, nw, s0, ymix_gla, **grp)
                acc.append(jnp.swapaxes(st.reshape(grp['nseq'], GLA_HEADS, GLA_DV, GLA_DK), 2, 3))
            ymix = ymix_gla
            wo = gla_w_out[j].astype(BF16)

        wr = _pad_cols(jnp.concatenate([moe_w_grp[i], moe_w_exp[i]], axis=1), LANES).astype(BF16)
        br = _pad_cols(jnp.concatenate([moe_b_grp[i], moe_b_exp[i]])[None], LANES)
        h1, h1b, ri, rw, cnt = _post_mixer(ymix, wo, h, ln1_g[i][None], ln1_b[i][None], wr, br, alpha=alpha)
        ya, yb = _moe(h1b, ri, cnt, moe_w_gate, moe_w_up, moe_w_down, i)
        h = _combine(h1, ya, yb, rw, ln2_g[i][None], ln2_b[i][None], alpha=alpha)

    y_prompt = h[:off_s].reshape(bp, lp_pad, D_MODEL)[:, N_META:lp]
    y_sample = h[off_s:off_s + bs * lsm].reshape(bs, lsm, D_MODEL)
    return (y_prompt, y_sample, jnp.stack(new_ssd_p), jnp.stack(new_conv_p), jnp.stack(new_gla_p),
            jnp.stack(new_ssd_s), jnp.stack(new_conv_s), jnp.stack(new_gla_s))
```

```python
import functools
import math

import jax
import jax.numpy as jnp
from jax import lax
from jax.experimental import pallas as pl
from jax.experimental.pallas import tpu as pltpu

F32 = jnp.float32
BF16 = jnp.bfloat16
I32 = jnp.int32
HI = lax.Precision.HIGHEST
NT = (((1,), (1,)), ((), ()))
TN = (((0,), (0,)), ((), ()))

D_MODEL = 1024
N_META = 16
SSD_INNER = 2048
SSD_HEAD_DIM = 64
SSD_HEADS = 32
SSD_GROUPS = 4
SSD_STATE = 128
SSD_CONV_W = 4
SSD_CONV_DIM = 3072
GLA_HEADS = 4
GLA_KEY_DIM = 512
GLA_VAL_DIM = 1024
GLA_DK = 128
GLA_DV = 256
GLA_GATE_RANK = 16
GLA_GATE_NORM = 16.0
MOE_GROUPS = 4
MOE_EXP_PER_GROUP = 8
MOE_EXPERTS = 32
EXPERT_FF = 512
NORM_EPS = 1e-5

LANES = 128
VMEM_LIMIT_MB = 56
ROW_TILE = 256
POST_TILE = 512
PROMPT_CHUNK = 128
GLA_SUB = 8
EXPERT_BLOCK = 256


def _params(sem):
    return pltpu.CompilerParams(dimension_semantics=sem, vmem_limit_bytes=VMEM_LIMIT_MB << 20)


def _sigmoid(x):
    return 1.0 / (1.0 + jnp.exp(-x))


def _softplus(x):
    return jnp.maximum(x, 0.0) + jnp.log1p(jnp.exp(-jnp.abs(x)))


def _iota(shape, dim):
    return lax.broadcasted_iota(I32, shape, dim)


def _proj_kernel(x_ref, *refs):
    n = len(refs) // 2
    xb = x_ref[...].astype(BF16)
    for w_ref, o_ref in zip(refs[:n], refs[n:]):
        o_ref[...] = jnp.dot(xb, w_ref[...], preferred_element_type=F32).astype(o_ref.dtype)


def _proj(x, ws, out_dtypes):
    rows, k = x.shape
    grid = (rows // ROW_TILE,)
    in_specs = [pl.BlockSpec((ROW_TILE, k), lambda i: (i, 0))]
    in_specs += [pl.BlockSpec(w.shape, lambda i: (0, 0)) for w in ws]
    out_specs = [pl.BlockSpec((ROW_TILE, w.shape[1]), lambda i: (i, 0)) for w in ws]
    out_shape = [jax.ShapeDtypeStruct((rows, w.shape[1]), dt) for w, dt in zip(ws, out_dtypes)]
    return pl.pallas_call(
        _proj_kernel, grid=grid, in_specs=in_specs, out_specs=out_specs, out_shape=out_shape,
        compiler_params=_params(("parallel",)))(x, *ws)


def _ssd_kernel(z_ref, xbc_ref, dt_ref, cw_ref, cb_ref, dtb_ref, an_ref, dsk_ref, nw_ref, s0_ref, c0_ref,
                yprev_ref, y_ref, sout_ref, st_ref, xc_ref, u_ref, *, C, L):
    del yprev_ref
    c = pl.program_id(1)
    hist = 8

    @pl.when(c == 0)
    def _():
        st_ref[...] = s0_ref[0]
        xc_ref[0:hist, :] = c0_ref[0]

    @pl.when(c > 0)
    def _():
        xc_ref[0:hist, :] = xc_ref[C:C + hist, :]

    xc_ref[hist:hist + C, :] = xbc_ref[...].astype(F32)

    for j in range(0, SSD_CONV_DIM, 512):
        sl = slice(j, j + 512)
        acc = cb_ref[:, sl] + xc_ref[hist:hist + C, sl] * cw_ref[3:4, sl]
        for w in range(SSD_CONV_W - 1):
            off = hist - (SSD_CONV_W - 1) + w
            acc = acc + xc_ref[off:off + C, sl] * cw_ref[w:w + 1, sl]
        u_ref[:, sl] = acc * _sigmoid(acc)

    row = _iota((C, LANES), 0) + c * C
    dt = jnp.where(row < L, _softplus(dt_ref[...] + dtb_ref[...]), 0.0)
    loga = dt * an_ref[...]
    ti = _iota((C, C), 0)
    si = _iota((C, C), 1)
    tril = si <= ti
    cum = jnp.dot(tril.astype(F32), loga, precision=HI, preferred_element_type=F32)
    eye = (_iota((LANES, LANES), 0) == _iota((LANES, LANES), 1)).astype(F32)
    cum_t = lax.dot_general(eye, cum, NT, precision=HI, preferred_element_type=F32)
    dt_t = lax.dot_general(eye, dt, NT, precision=HI, preferred_element_type=F32)
    w_t = dt_t * jnp.exp(cum_t[:, C - 1:C] - cum_t)
    src_t = cum_t - jnp.log(dt_t)
    e_last = jnp.exp(cum[C - 1:C, :])
    eye_b = eye.astype(BF16)
    lane = _iota((C, LANES), 1)
    lane_s = _iota((SSD_STATE, LANES), 1)
    lane_r = _iota((1, LANES), 1)
    hpg = SSD_HEADS // SSD_GROUPS

    for g in range(SSD_GROUPS):
        b_f = u_ref[:, SSD_INNER + g * SSD_STATE:SSD_INNER + (g + 1) * SSD_STATE]
        c_f = u_ref[:, SSD_INNER + (SSD_GROUPS + g) * SSD_STATE:SSD_INNER + (SSD_GROUPS + g + 1) * SSD_STATE]
        b_b = b_f.astype(BF16)
        cb = lax.dot_general(c_f.astype(BF16), b_b, NT, preferred_element_type=F32)
        b_t = lax.dot_general(eye_b, b_b, NT, preferred_element_type=F32)
        ypairs = []
        for jp in range(hpg // 2):
            p = g * (hpg // 2) + jp
            cs = slice(p * LANES, (p + 1) * LANES)
            xs_p = u_ref[:, cs]
            xs_b = xs_p.astype(BF16)
            st_p = st_ref[:, cs]
            st_b = st_p.astype(BF16)
            res, upd = [], []
            for h in (2 * p, 2 * p + 1):
                ccol = cum[:, h:h + 1]
                dec = jnp.where(tril, jnp.exp(ccol - src_t[h:h + 1, :]), 0.0)
                wmat = (cb * dec).astype(BF16)
                c_e = (c_f * jnp.exp(ccol)).astype(BF16)
                res.append(jnp.dot(wmat, xs_b, preferred_element_type=F32)
                           + jnp.dot(c_e, st_b, preferred_element_type=F32))
                b_w = (b_t * w_t[h:h + 1, :]).astype(BF16)
                upd.append(jnp.dot(b_w, xs_b, preferred_element_type=F32))
            y_p = jnp.where(lane < SSD_HEAD_DIM, res[0], res[1]) + dsk_ref[:, cs] * xs_p
            e_p = jnp.where(lane_r < SSD_HEAD_DIM, e_last[:, 2 * p:2 * p + 1], e_last[:, 2 * p + 1:2 * p + 2])
            st_ref[:, cs] = st_p * e_p + jnp.where(lane_s < SSD_HEAD_DIM, upd[0], upd[1])
            ypairs.append(y_p)
        gs = slice(g * 512, (g + 1) * 512)
        yg = jnp.concatenate(ypairs, axis=1)
        zg = z_ref[:, gs].astype(F32)
        yg = yg * (zg * _sigmoid(zg))
        ms = jnp.mean(yg * yg, axis=-1, keepdims=True)
        y_ref[:, gs] = (yg * lax.rsqrt(ms + NORM_EPS) * nw_ref[:, gs]).astype(y_ref.dtype)

    @pl.when(c == pl.num_programs(1) - 1)
    def _():
        sout_ref[0] = st_ref[...]


def _ssd_scan(z, xbc, dt, cw, cb, dtb, an, dsk, nw, s0_t, c0, y_prev, *, row_off, nseq, nwalk, L, Lpad, C):
    nc = Lpad // C
    base = row_off // C

    def rmap(b, c):
        return (base + b * nc + c, 0)

    const = lambda b, c: (0, 0)
    in_specs = [
        pl.BlockSpec((C, SSD_INNER), rmap),
        pl.BlockSpec((C, SSD_CONV_DIM), rmap),
        pl.BlockSpec((C, LANES), rmap),
        pl.BlockSpec((SSD_CONV_W, SSD_CONV_DIM), const),
        pl.BlockSpec((1, SSD_CONV_DIM), const),
        pl.BlockSpec((1, LANES), const),
        pl.BlockSpec((1, LANES), const),
        pl.BlockSpec((1, SSD_INNER), const),
        pl.BlockSpec((1, SSD_INNER), const),
        pl.BlockSpec((1, SSD_STATE, SSD_INNER), lambda b, c: (jnp.minimum(b, nseq - 1), 0, 0)),
        pl.BlockSpec((1, 8, SSD_CONV_DIM), lambda b, c: (jnp.minimum(b, nseq - 1), 0, 0)),
        pl.BlockSpec(memory_space=pl.ANY),
    ]
    out_specs = [
        pl.BlockSpec((C, SSD_INNER), rmap),
        pl.BlockSpec((1, SSD_STATE, SSD_INNER), lambda b, c: (b, 0, 0)),
    ]
    out_shape = [
        jax.ShapeDtypeStruct(y_prev.shape, y_prev.dtype),
        jax.ShapeDtypeStruct((nwalk, SSD_STATE, SSD_INNER), F32),
    ]
    scratch = [
        pltpu.VMEM((SSD_STATE, SSD_INNER), F32),
        pltpu.VMEM((C + 8, SSD_CONV_DIM), F32),
        pltpu.VMEM((C, SSD_CONV_DIM), F32),
    ]
    y, st = pl.pallas_call(
        functools.partial(_ssd_kernel, C=C, L=L), grid=(nwalk, nc), in_specs=in_specs, out_specs=out_specs,
        out_shape=out_shape, scratch_shapes=scratch, input_output_aliases={11: 0},
        compiler_params=_params(("arbitrary", "arbitrary")))(z, xbc, dt, cw, cb, dtb, an, dsk, nw, s0_t, c0, y_prev)
    return y, st[:nseq]


def _gla_kernel(q_ref, k_ref, v_ref, g_ref, lr_ref, w2_ref, b2_ref, nw_ref, s0_ref,
                oprev_ref, o_ref, sout_ref, st_ref, *, C, L):
    del oprev_ref
    c = pl.program_id(1)
    sub = GLA_SUB
    nb = C // sub

    @pl.when(c == 0)
    def _():
        st_ref[...] = s0_ref[0]

    row_k = _iota((C, GLA_KEY_DIM), 0) + c * C
    gk = jnp.dot(lr_ref[...].astype(BF16), w2_ref[...], preferred_element_type=F32) + b2_ref[...]
    logg = (jnp.minimum(gk, 0.0) - jnp.log1p(jnp.exp(-jnp.abs(gk)))) * (1.0 / GLA_GATE_NORM)
    logg = jnp.where(row_k < L, logg, 0.0)
    ti = _iota((C, C), 0)
    si = _iota((C, C), 1)
    tril = si <= ti
    cum_all = jnp.dot(tril.astype(F32), logg, precision=HI, preferred_element_type=F32)
    same_sub = (ti // sub) == (si // sub)
    diag_mask = jnp.logical_and(tril, same_sub)
    sel = ((_iota((sub * GLA_DK, C), 0) // GLA_DK) == (_iota((sub * GLA_DK, C), 1) % sub)).astype(BF16)
    row_v = _iota((C, GLA_DV), 0) + c * C
    valid_v = row_v < L
    scale = GLA_DK ** -0.5

    for h in range(GLA_HEADS):
        ks = slice(h * GLA_DK, (h + 1) * GLA_DK)
        vs = slice(h * GLA_DV, (h + 1) * GLA_DV)
        q = q_ref[:, ks].astype(F32) * scale
        k = k_ref[:, ks].astype(F32)
        v = jnp.where(valid_v, v_ref[:, vs].astype(F32), 0.0)
        v_b = v.astype(BF16)
        cum = cum_all[:, ks]
        st = st_ref[h * GLA_DV:(h + 1) * GLA_DV, :]
        q_e = (q * jnp.exp(cum)).astype(BF16)
        y = lax.dot_general(q_e, st.astype(BF16), NT, preferred_element_type=F32)
        if nb > 1:
            q_parts, k_parts = [], []
            for i in range(1, nb):
                r_i = cum[i * sub - 1:i * sub, :]
                qi = q[i * sub:(i + 1) * sub, :] * jnp.exp(cum[i * sub:(i + 1) * sub, :] - r_i)
                q_parts.append(jnp.concatenate(
                    [jnp.zeros((i * sub, GLA_DK), F32), qi, jnp.zeros((C - (i + 1) * sub, GLA_DK), F32)], axis=0)
                    if C - (i + 1) * sub > 0 else jnp.concatenate([jnp.zeros((i * sub, GLA_DK), F32), qi], axis=0))
                ki = k[0:i * sub, :] * jnp.exp(r_i - cum[0:i * sub, :])
                k_parts.append(jnp.concatenate([ki, jnp.zeros((C - i * sub, GLA_DK), F32)], axis=0))
            q_st = jnp.concatenate(q_parts, axis=1).astype(BF16)
            k_st = jnp.concatenate(k_parts, axis=1).astype(BF16)
            a_off = lax.dot_general(q_st, k_st, NT, preferred_element_type=F32)
        else:
            a_off = jnp.zeros((C, C), F32)
        q3 = q.reshape(nb, sub, GLA_DK)
        k3 = k.reshape(nb, sub, GLA_DK)
        c3 = cum.reshape(nb, sub, GLA_DK)
        p_parts = []
        for o in range(sub):
            p_o = q3 * jnp.exp(jnp.minimum(c3 - c3[:, o:o + 1, :], 0.0)) * k3[:, o:o + 1, :]
            p_parts.append(p_o.reshape(C, GLA_DK).astype(BF16))
        p_st = jnp.concatenate(p_parts, axis=1)
        a_diag = jnp.dot(p_st, sel, preferred_element_type=F32)
        a = a_off + jnp.where(diag_mask, a_diag, 0.0)
        y = y + jnp.dot(a.astype(BF16), v_b, preferred_element_type=F32)
        last = cum[C - 1:C, :]
        k_e = (k * jnp.exp(last - cum)).astype(BF16)
        st_ref[h * GLA_DV:(h + 1) * GLA_DV, :] = (
            st * jnp.exp(last) + lax.dot_general(v_b, k_e, TN, preferred_element_type=F32))
        ms = jnp.mean(y * y, axis=-1, keepdims=True)
        gate = g_ref[:, vs].astype(F32)
        o_ref[:, vs] = (y * lax.rsqrt(ms + NORM_EPS) * nw_ref[...] * (gate * _sigmoid(gate))).astype(o_ref.dtype)

    @pl.when(c == pl.num_programs(1) - 1)
    def _():
        sout_ref[0] = st_ref[...]


def _gla_scan(qkvg, lr, w2, b2, nw, s0_t, o_prev, *, row_off, nseq, nwalk, L, Lpad, C):
    nc = Lpad // C
    base = row_off // C
    const = lambda b, c: (0, 0)
    in_specs = [
        pl.BlockSpec((C, GLA_KEY_DIM), lambda b, c: (base + b * nc + c, 0)),
        pl.BlockSpec((C, GLA_KEY_DIM), lambda b, c: (base + b * nc + c, 1)),
        pl.BlockSpec((C, GLA_VAL_DIM), lambda b, c: (base + b * nc + c, 1)),
        pl.BlockSpec((C, GLA_VAL_DIM), lambda b, c: (base + b * nc + c, 2)),
        pl.BlockSpec((C, LANES), lambda b, c: (base + b * nc + c, 0)),
        pl.BlockSpec((LANES, GLA_KEY_DIM), const),
        pl.BlockSpec((1, GLA_KEY_DIM), const),
        pl.BlockSpec((1, GLA_DV), const),
        pl.BlockSpec((1, GLA_VAL_DIM, GLA_DK), lambda b, c: (jnp.minimum(b, nseq - 1), 0, 0)),
        pl.BlockSpec(memory_space=pl.ANY),
    ]
    out_specs = [
        pl.BlockSpec((C, GLA_VAL_DIM), lambda b, c: (base + b * nc + c, 0)),
        pl.BlockSpec((1, GLA_VAL_DIM, GLA_DK), lambda b, c: (b, 0, 0)),
    ]
    out_shape = [
        jax.ShapeDtypeStruct(o_prev.shape, o_prev.dtype),
        jax.ShapeDtypeStruct((nwalk, GLA_VAL_DIM, GLA_DK), F32),
    ]
    o, st = pl.pallas_call(
        functools.partial(_gla_kernel, C=C, L=L), grid=(nwalk, nc), in_specs=in_specs, out_specs=out_specs,
        out_shape=out_shape, scratch_shapes=[pltpu.VMEM((GLA_VAL_DIM, GLA_DK), F32)],
        input_output_aliases={9: 0},
        compiler_params=_params(("arbitrary", "arbitrary")))(qkvg, qkvg, qkvg, qkvg, lr, w2, b2, nw, s0_t, o_prev)
    return o, st[:nseq]


def _layer_norm(x, g, b):
    mu = jnp.mean(x, axis=-1, keepdims=True)
    xc = x - mu
    var = jnp.mean(xc * xc, axis=-1, keepdims=True)
    return xc * lax.rsqrt(var + NORM_EPS) * g + b


def _post_kernel(y_ref, wo_ref, h_ref, g_ref, b_ref, wr_ref, br_ref,
                 h1_ref, h1b_ref, ri_ref, rw_ref, cnt_ref, carry_ref, *, alpha):
    i = pl.program_id(0)
    tm = h_ref.shape[0]

    @pl.when(i == 0)
    def _():
        carry_ref[...] = jnp.zeros_like(carry_ref)

    mix = jnp.dot(y_ref[...], wo_ref[...], preferred_element_type=F32)
    h1 = _layer_norm(alpha * h_ref[...] + mix, g_ref[...], b_ref[...])
    h1_ref[...] = h1
    h_hi = h1.astype(BF16)
    h1b_ref[...] = h_hi
    logits = jnp.dot(h_hi, wr_ref[...], preferred_element_type=F32) + br_ref[...]
    lane = _iota((tm, LANES), 1)
    neg = -jnp.inf
    is_g = lane < MOE_GROUPS
    gl = jnp.where(is_g, logits, neg)
    gmax = jnp.max(gl, axis=-1, keepdims=True)
    gsel = jnp.min(jnp.where(gl == gmax, lane, LANES), axis=-1, keepdims=True)
    p_g = 1.0 / jnp.sum(jnp.where(is_g, jnp.exp(gl - gmax), 0.0), axis=-1, keepdims=True)
    eid = lane - MOE_GROUPS
    in_g = (eid >= 0) & (eid < MOE_EXPERTS) & ((eid // MOE_EXP_PER_GROUP) == gsel)
    el = jnp.where(in_g, logits, neg)
    v1 = jnp.max(el, axis=-1, keepdims=True)
    i1 = jnp.min(jnp.where(el == v1, lane, LANES), axis=-1, keepdims=True)
    el2 = jnp.where(lane == i1, neg, el)
    v2 = jnp.max(el2, axis=-1, keepdims=True)
    i2 = jnp.min(jnp.where(el2 == v2, lane, LANES), axis=-1, keepdims=True)
    t = jnp.exp(v2 - v1)
    w1 = p_g / (1.0 + t)
    w2 = p_g * t / (1.0 + t)
    e1 = i1 - MOE_GROUPS
    e2 = i2 - MOE_GROUPS
    oh = jnp.where(lane == e1, 1.0, 0.0) + jnp.where(lane == e2, 1.0, 0.0)
    strict = (_iota((tm, tm), 1) < _iota((tm, tm), 0)).astype(BF16)
    before = jnp.dot(strict, oh.astype(BF16), preferred_element_type=F32) + carry_ref[...]
    r1 = jnp.sum(jnp.where(lane == e1, before, 0.0), axis=-1, keepdims=True)
    r2 = jnp.sum(jnp.where(lane == e2, before, 0.0), axis=-1, keepdims=True)
    carry_ref[...] = carry_ref[...] + jnp.sum(oh, axis=0, keepdims=True)
    cnt_ref[...] = carry_ref[...]
    packed = jnp.where(lane == 0, e1.astype(F32), jnp.where(lane == 1, e2.astype(F32), jnp.where(
        lane == 2, r1, jnp.where(lane == 3, r2, 0.0))))
    ri_ref[...] = packed.T[0:8, :]
    rw_ref[...] = jnp.where(lane == 0, w1, jnp.where(lane == 1, w2, 0.0))


def _post_mixer(y, wo, h, g, b, wr, br, *, alpha):
    rows, kin = y.shape
    grid = (rows // POST_TILE,)
    rmap = lambda i: (i, 0)
    const = lambda i: (0, 0)
    in_specs = [
        pl.BlockSpec((POST_TILE, kin), rmap),
        pl.BlockSpec((kin, D_MODEL), const),
        pl.BlockSpec((POST_TILE, D_MODEL), rmap),
        pl.BlockSpec((1, D_MODEL), const),
        pl.BlockSpec((1, D_MODEL), const),
        pl.BlockSpec((D_MODEL, LANES), const),
        pl.BlockSpec((1, LANES), const),
    ]
    out_specs = [
        pl.BlockSpec((POST_TILE, D_MODEL), rmap),
        pl.BlockSpec((POST_TILE, D_MODEL), rmap),
        pl.BlockSpec((8, POST_TILE), lambda i: (0, i)),
        pl.BlockSpec((POST_TILE, LANES), rmap),
        pl.BlockSpec((1, LANES), const),
    ]
    out_shape = [
        jax.ShapeDtypeStruct((rows, D_MODEL), F32),
        jax.ShapeDtypeStruct((rows, D_MODEL), BF16),
        jax.ShapeDtypeStruct((8, rows), F32),
        jax.ShapeDtypeStruct((rows, LANES), F32),
        jax.ShapeDtypeStruct((1, LANES), F32),
    ]
    return pl.pallas_call(
        functools.partial(_post_kernel, alpha=alpha), grid=grid, in_specs=in_specs, out_specs=out_specs,
        out_shape=out_shape, scratch_shapes=[pltpu.VMEM((1, LANES), F32)],
        compiler_params=_params(("arbitrary",)))(y, wo, h, g, b, wr, br)


def _expert_kernel(be_ref, nu_ref, x_ref, wg_ref, wu_ref, wd_ref, o_ref, wg_b, wu_b, wd_b):
    i = pl.program_id(0)
    prev = be_ref[jnp.maximum(i - 1, 0)]

    @pl.when(jnp.logical_or(i == 0, be_ref[i] != prev))
    def _():
        wg_b[...] = wg_ref[...].astype(BF16)
        wu_b[...] = wu_ref[...].astype(BF16)
        wd_b[...] = wd_ref[...].astype(BF16)

    @pl.when(i < nu_ref[0])
    def _():
        x = x_ref[...]
        hg = jnp.dot(x, wg_b[...], preferred_element_type=F32)
        hu = jnp.dot(x, wu_b[...], preferred_element_type=F32)
        hh = (hg * _sigmoid(hg) * hu).astype(BF16)
        o_ref[...] = jnp.dot(hh, wd_b[...], preferred_element_type=F32).astype(o_ref.dtype)

    @pl.when(i >= nu_ref[0])
    def _():
        o_ref[...] = jnp.zeros_like(o_ref)


def _experts(block_e, n_used, xb, w_gate, w_up, w_down, layer):
    nblk = xb.shape[0] // EXPERT_BLOCK
    wmap = lambda i, be, nu: (layer, be[i], 0, 0)
    grid_spec = pltpu.PrefetchScalarGridSpec(
        num_scalar_prefetch=2, grid=(nblk,),
        in_specs=[
            pl.BlockSpec((EXPERT_BLOCK, D_MODEL), lambda i, be, nu: (i, 0)),
            pl.BlockSpec((None, None, D_MODEL, EXPERT_FF), wmap),
            pl.BlockSpec((None, None, D_MODEL, EXPERT_FF), wmap),
            pl.BlockSpec((None, None, EXPERT_FF, D_MODEL), wmap),
        ],
        out_specs=pl.BlockSpec((EXPERT_BLOCK, D_MODEL), lambda i, be, nu: (i, 0)),
        scratch_shapes=[pltpu.VMEM((D_MODEL, EXPERT_FF), BF16), pltpu.VMEM((D_MODEL, EXPERT_FF), BF16),
                        pltpu.VMEM((EXPERT_FF, D_MODEL), BF16)])
    return pl.pallas_call(
        _expert_kernel, grid_spec=grid_spec,
        out_shape=jax.ShapeDtypeStruct((nblk * EXPERT_BLOCK, D_MODEL), BF16),
        compiler_params=_params(("arbitrary",)))(block_e, n_used, xb, w_gate, w_up, w_down)


def _combine_kernel(h_ref, ya_ref, yb_ref, rw_ref, g_ref, b_ref, o_ref, *, alpha):
    rw = rw_ref[...]
    ffn = ya_ref[...].astype(F32) * rw[:, 0:1] + yb_ref[...].astype(F32) * rw[:, 1:2]
    o_ref[...] = _layer_norm(alpha * h_ref[...] + ffn, g_ref[...], b_ref[...])


def _combine(h1, ya, yb, rw, g, b, *, alpha):
    rows = h1.shape[0]
    rmap = lambda i: (i, 0)
    const = lambda i: (0, 0)
    in_specs = [
        pl.BlockSpec((POST_TILE, D_MODEL), rmap), pl.BlockSpec((POST_TILE, D_MODEL), rmap),
        pl.BlockSpec((POST_TILE, D_MODEL), rmap), pl.BlockSpec((POST_TILE, LANES), rmap),
        pl.BlockSpec((1, D_MODEL), const), pl.BlockSpec((1, D_MODEL), const),
    ]
    return pl.pallas_call(
        functools.partial(_combine_kernel, alpha=alpha), grid=(rows // POST_TILE,), in_specs=in_specs,
        out_specs=pl.BlockSpec((POST_TILE, D_MODEL), rmap),
        out_shape=jax.ShapeDtypeStruct((rows, D_MODEL), F32),
        compiler_params=_params(("parallel",)))(h1, ya, yb, rw, g, b)


def _round_up(x, m):
    return (x + m - 1) // m * m


def _pad_cols(w, n):
    return jnp.pad(w, ((0, 0), (0, n - w.shape[1])))


def _take_rows(x, idx):
    return x.at[idx].get(mode='promise_in_bounds')


def _moe(h1b, ri, cnt, w_gate, w_up, w_down, layer):
    rows = h1b.shape[0]
    nblk = (2 * rows + MOE_EXPERTS * (EXPERT_BLOCK - 1) + EXPERT_BLOCK - 1) // EXPERT_BLOCK
    counts = cnt[0, :MOE_EXPERTS].astype(I32)
    pcounts = (counts + EXPERT_BLOCK - 1) // EXPERT_BLOCK * EXPERT_BLOCK
    pends = jnp.cumsum(pcounts)
    pstarts = pends - pcounts
    ri = ri.astype(I32)
    dest = _take_rows(pstarts, ri[0:2].reshape(-1)).reshape(2, rows) + ri[2:4]
    tok = jnp.broadcast_to(jnp.arange(rows, dtype=I32)[None, :], (2, rows))
    row_tok = (jnp.arange(nblk * EXPERT_BLOCK, dtype=I32) % rows).at[dest.reshape(-1)].set(
        tok.reshape(-1), mode='promise_in_bounds', unique_indices=True)
    blk_start = jnp.arange(nblk, dtype=I32) * EXPERT_BLOCK
    block_e = jnp.minimum(jnp.sum((pends[None, :] <= blk_start[:, None]).astype(I32), axis=1), MOE_EXPERTS - 1)
    n_used = (pends[-1:] // EXPERT_BLOCK).astype(I32)
    xb = _take_rows(h1b, row_tok)
    yb = _experts(block_e, n_used, xb, w_gate, w_up, w_down, layer)
    return _take_rows(yb, dest[0]), _take_rows(yb, dest[1])


def kernel(x_prompt, x_sample, state_ssd, state_ssd_conv, state_gla, meta_tokens, ssd_w_in, ssd_conv_w, ssd_conv_b, ssd_dt_bias, ssd_A_log, ssd_D, ssd_norm_w, ssd_w_out, gla_w_in, gla_w_gk2, gla_b_gk2, gla_norm_w, gla_w_out, ln1_g, ln1_b, moe_w_grp, moe_b_grp, moe_w_exp, moe_b_exp, moe_w_gate, moe_w_up, moe_w_down, ln2_g, ln2_b):
    bp, seq, _ = x_prompt.shape
    bs, lsm, _ = x_sample.shape
    depth = ln1_g.shape[0]
    alpha = (2.0 * depth) ** 0.25
    lp = N_META + seq
    cp = PROMPT_CHUNK
    lp_pad = _round_up(lp, cp)
    cs = lsm
    off_s = bp * lp_pad
    assert off_s % cs == 0 and cs % GLA_SUB == 0 and cp % GLA_SUB == 0
    rows = _round_up(off_s + bs * lsm, POST_TILE)

    pieces = []
    for b in range(bp):
        pieces += [meta_tokens.astype(F32), x_prompt[b], jnp.zeros((lp_pad - lp, D_MODEL), F32)]
    pieces += [x_sample.reshape(bs * lsm, D_MODEL), jnp.zeros((rows - off_s - bs * lsm, D_MODEL), F32)]
    h = jnp.concatenate(pieces, axis=0)

    groups = (dict(row_off=0, nseq=bp, nwalk=bp, L=lp, Lpad=lp_pad, C=cp),
              dict(row_off=off_s, nseq=bs, nwalk=(rows - off_s) // cs, L=lsm, Lpad=lsm, C=cs))
    ymix_ssd = jnp.zeros((rows, SSD_INNER), BF16)
    ymix_gla = jnp.zeros((rows, GLA_VAL_DIM), BF16)

    new_ssd_p, new_ssd_s, new_conv_p, new_conv_s, new_gla_p, new_gla_s = [], [], [], [], [], []
    for i in range(depth):
        j = i // 2
        if i % 2 == 0:
            w_in = ssd_w_in[j].astype(BF16)
            z, xbc, dtr = _proj(
                h, [w_in[:, :SSD_INNER], w_in[:, SSD_INNER:SSD_INNER + SSD_CONV_DIM],
                    _pad_cols(w_in[:, SSD_INNER + SSD_CONV_DIM:], LANES)], [F32, F32, F32])
            cw = ssd_conv_w[j]
            cb = ssd_conv_b[j][None]
            dtb = _pad_cols(ssd_dt_bias[j][None], LANES)
            an = _pad_cols(-jnp.exp(ssd_A_log[j].astype(F32))[None], LANES)
            dsk = jnp.repeat(ssd_D[j], SSD_HEAD_DIM)[None]
            nw = ssd_norm_w[j][None]
            s0s = [jnp.zeros((bp, SSD_STATE, SSD_INNER), F32),
                   jnp.swapaxes(state_ssd[j].reshape(bs, SSD_INNER, SSD_STATE), 1, 2)]
            c0s = [jnp.zeros((bp, 8, SSD_CONV_DIM), F32),
                   jnp.pad(state_ssd_conv[j], ((0, 0), (8 - (SSD_CONV_W - 1), 0), (0, 0)))]
            for grp, s0, c0, acc_s, acc_c in zip(groups, s0s, c0s, (new_ssd_p, new_ssd_s), (new_conv_p, new_conv_s)):
                ymix_ssd, st = _ssd_scan(z, xbc, dtr, cw, cb, dtb, an, dsk, nw, s0, c0, ymix_ssd, **grp)
                n, L, Lpad = grp['nseq'], grp['L'], grp['Lpad']
                acc_s.append(jnp.swapaxes(st, 1, 2).reshape(n, SSD_HEADS, SSD_HEAD_DIM, SSD_STATE))
                nconv = SSD_CONV_W - 1
                last = (grp['row_off'] + jnp.arange(n, dtype=I32)[:, None] * Lpad + (L - nconv)
                        + jnp.arange(nconv, dtype=I32)[None, :])
                acc_c.append(_take_rows(xbc, last.reshape(-1)).reshape(n, nconv, SSD_CONV_DIM))
            ymix = ymix_ssd
            wo = ssd_w_out[j].astype(BF16)
        else:
            w_in = gla_w_in[j].astype(BF16)
            nq = 2 * GLA_KEY_DIM + 2 * GLA_VAL_DIM
            qkvg, lr = _proj(h, [w_in[:, :nq], _pad_cols(w_in[:, nq:], LANES)], [F32, F32])
            w2 = jnp.pad(gla_w_gk2[j], ((0, LANES - GLA_GATE_RANK), (0, 0))).astype(BF16)
            b2 = gla_b_gk2[j][None]
            nw = gla_norm_w[j][None]
            s0s = [jnp.zeros((bp, GLA_VAL_DIM, GLA_DK), F32),
                   jnp.swapaxes(state_gla[j], 2, 3).reshape(bs, GLA_VAL_DIM, GLA_DK)]
            for grp, s0, acc in zip(groups, s0s, (new_gla_p, new_gla_s)):
                ymix_gla, st = _gla_scan(qkvg, lr, w2, b2, nw, s0, ymix_gla, **grp)
                acc.append(jnp.swapaxes(st.reshape(grp['nseq'], GLA_HEADS, GLA_DV, GLA_DK), 2, 3))
            ymix = ymix_gla
            wo = gla_w_out[j].astype(BF16)

        wr = _pad_cols(jnp.concatenate([moe_w_grp[i], moe_w_exp[i]], axis=1), LANES).astype(BF16)
        br = _pad_cols(jnp.concatenate([moe_b_grp[i], moe_b_exp[i]])[None], LANES)
        h1, h1b, ri, rw, cnt = _post_mixer(ymix, wo, h, ln1_g[i][None], ln1_b[i][None], wr, br, alpha=alpha)
        ya, yb = _moe(h1b, ri, cnt, moe_w_gate, moe_w_up, moe_w_down, i)
        h = _combine(h1, ya, yb, rw, ln2_g[i][None], ln2_b[i][None], alpha=alpha)

    y_prompt = h[:off_s].reshape(bp, lp_pad, D_MODEL)[:, N_META:lp]
    y_sample = h[off_s:off_s + bs * lsm].reshape(bs, lsm, D_MODEL)
    return (y_prompt, y_sample, jnp.stack(new_ssd_p), jnp.stack(new_conv_p), jnp.stack(new_gla_p),
            jnp.stack(new_ssd_s), jnp.stack(new_conv_s), jnp.stack(new_gla_s))
```

```python
import functools
import math

import jax
import jax.numpy as jnp
from jax import lax
from jax.experimental import pallas as pl
from jax.experimental.pallas import tpu as pltpu

F32 = jnp.float32
BF16 = jnp.bfloat16
I32 = jnp.int32
NT = (((1,), (1,)), ((), ()))
TN = (((0,), (0,)), ((), ()))
NN = (((1,), (0,)), ((), ()))

D_MODEL = 1024
N_META = 16
SSD_INNER = 2048
SSD_HEAD_DIM = 64
SSD_HEADS = 32
SSD_GROUPS = 4
SSD_STATE = 128
SSD_CONV_W = 4
SSD_CONV_DIM = 3072
GLA_HEADS = 4
GLA_KEY_DIM = 512
GLA_VAL_DIM = 1024
GLA_DK = 128
GLA_DV = 256
GLA_GATE_RANK = 16
GLA_GATE_NORM = 16.0
MOE_GROUPS = 4
MOE_EXP_PER_GROUP = 8
MOE_EXPERTS = 32
EXPERT_FF = 512
NORM_EPS = 1e-5

LANES = 128
VMEM_LIMIT_MB = 56
ROW_TILE = 256
POST_TILE = 512
PROMPT_CHUNK = 128
PROMPT_STEP_CHUNKS = 2
GLA_SUB = 8
EXPERT_BLOCK = 256


def _params(sem):
    return pltpu.CompilerParams(dimension_semantics=sem, vmem_limit_bytes=VMEM_LIMIT_MB << 20)


def _sigmoid(x):
    return 1.0 / (1.0 + jnp.exp(-x))


def _softplus(x):
    return jnp.maximum(x, 0.0) + jnp.log1p(jnp.exp(-jnp.abs(x)))


def _iota(shape, dim):
    return lax.broadcasted_iota(I32, shape, dim)


def _exact_dot(a_b, x, dims):
    hi = x.astype(BF16)
    r1 = x - hi.astype(F32)
    mid = r1.astype(BF16)
    lo = (r1 - mid.astype(F32)).astype(BF16)
    f = lambda t: lax.dot_general(a_b, t, dims, preferred_element_type=F32)
    return (f(hi) + f(mid)) + f(lo)


def _proj_kernel(x_ref, *refs):
    n = len(refs) // 2
    xb = x_ref[...].astype(BF16)
    for w_ref, o_ref in zip(refs[:n], refs[n:]):
        o_ref[...] = jnp.dot(xb, w_ref[...], preferred_element_type=F32).astype(o_ref.dtype)


def _proj(x, ws, out_dtypes):
    rows, k = x.shape
    grid = (rows // ROW_TILE,)
    in_specs = [pl.BlockSpec((ROW_TILE, k), lambda i: (i, 0))]
    in_specs += [pl.BlockSpec(w.shape, lambda i: (0, 0)) for w in ws]
    out_specs = [pl.BlockSpec((ROW_TILE, w.shape[1]), lambda i: (i, 0)) for w in ws]
    out_shape = [jax.ShapeDtypeStruct((rows, w.shape[1]), dt) for w, dt in zip(ws, out_dtypes)]
    return pl.pallas_call(
        _proj_kernel, grid=grid, in_specs=in_specs, out_specs=out_specs, out_shape=out_shape,
        compiler_params=_params(("parallel",)))(x, *ws)


def _ssd_kernel(z_ref, xbc_ref, dt_ref, cw_ref, cb_ref, dtb_ref, an_ref, dsk_ref, nw_ref, s0_ref, c0_ref,
                yprev_ref, y_ref, sout_ref, st_ref, xc_ref, u_ref, *, C, L):
    del yprev_ref
    c = pl.program_id(1)
    hist = 8

    @pl.when(c == 0)
    def _():
        st_ref[...] = s0_ref[0]
        xc_ref[0:hist, :] = c0_ref[0]

    @pl.when(c > 0)
    def _():
        xc_ref[0:hist, :] = xc_ref[C:C + hist, :]

    xc_ref[hist:hist + C, :] = xbc_ref[...].astype(F32)

    for j in range(0, SSD_CONV_DIM, 512):
        sl = slice(j, j + 512)
        acc = cb_ref[:, sl] + xc_ref[hist:hist + C, sl] * cw_ref[3:4, sl]
        for w in range(SSD_CONV_W - 1):
            off = hist - (SSD_CONV_W - 1) + w
            acc = acc + xc_ref[off:off + C, sl] * cw_ref[w:w + 1, sl]
        u_ref[:, sl] = acc * _sigmoid(acc)

    row = _iota((C, LANES), 0) + c * C
    dt = jnp.where(row < L, _softplus(dt_ref[...] + dtb_ref[...]), 0.0)
    loga = dt * an_ref[...]
    ti = _iota((C, C), 0)
    si = _iota((C, C), 1)
    tril = si <= ti
    eye_b = (_iota((LANES, LANES), 0) == _iota((LANES, LANES), 1)).astype(BF16)
    cum = _exact_dot(tril.astype(BF16), loga, NN)
    cum_t = _exact_dot(eye_b, cum, NT)
    dt_t = _exact_dot(eye_b, dt, NT)
    w_t = dt_t * jnp.exp(cum_t[:, C - 1:C] - cum_t)
    src_t = cum_t - jnp.log(dt_t)
    e_last = jnp.exp(cum[C - 1:C, :])
    lane = _iota((C, LANES), 1)
    lane_s = _iota((SSD_STATE, LANES), 1)
    lane_r = _iota((1, LANES), 1)
    hpg = SSD_HEADS // SSD_GROUPS

    for g in range(SSD_GROUPS):
        b_f = u_ref[:, SSD_INNER + g * SSD_STATE:SSD_INNER + (g + 1) * SSD_STATE]
        c_f = u_ref[:, SSD_INNER + (SSD_GROUPS + g) * SSD_STATE:SSD_INNER + (SSD_GROUPS + g + 1) * SSD_STATE]
        b_b = b_f.astype(BF16)
        cb = lax.dot_general(c_f.astype(BF16), b_b, NT, preferred_element_type=F32)
        b_t = lax.dot_general(eye_b, b_b, NT, preferred_element_type=F32)
        ypairs = []
        for jp in range(hpg // 2):
            p = g * (hpg // 2) + jp
            cs = slice(p * LANES, (p + 1) * LANES)
            xs_p = u_ref[:, cs]
            xs_b = xs_p.astype(BF16)
            st_p = st_ref[:, cs]
            st_b = st_p.astype(BF16)
            res, upd = [], []
            for h in (2 * p, 2 * p + 1):
                ccol = cum[:, h:h + 1]
                dec = jnp.where(tril, jnp.exp(ccol - src_t[h:h + 1, :]), 0.0)
                wmat = (cb * dec).astype(BF16)
                c_e = (c_f * jnp.exp(ccol)).astype(BF16)
                res.append(jnp.dot(wmat, xs_b, preferred_element_type=F32)
                           + jnp.dot(c_e, st_b, preferred_element_type=F32))
                b_w = (b_t * w_t[h:h + 1, :]).astype(BF16)
                upd.append(jnp.dot(b_w, xs_b, preferred_element_type=F32))
            y_p = jnp.where(lane < SSD_HEAD_DIM, res[0], res[1]) + dsk_ref[:, cs] * xs_p
            e_p = jnp.where(lane_r < SSD_HEAD_DIM, e_last[:, 2 * p:2 * p + 1], e_last[:, 2 * p + 1:2 * p + 2])
            st_ref[:, cs] = st_p * e_p + jnp.where(lane_s < SSD_HEAD_DIM, upd[0], upd[1])
            ypairs.append(y_p)
        gs = slice(g * 512, (g + 1) * 512)
        yg = jnp.concatenate(ypairs, axis=1)
        zg = z_ref[:, gs].astype(F32)
        yg = yg * (zg * _sigmoid(zg))
        ms = jnp.mean(yg * yg, axis=-1, keepdims=True)
        y_ref[:, gs] = (yg * lax.rsqrt(ms + NORM_EPS) * nw_ref[:, gs]).astype(y_ref.dtype)

    @pl.when(c == pl.num_programs(1) - 1)
    def _():
        sout_ref[0] = st_ref[...]


def _ssd_scan(z, xbc, dt, cw, cb, dtb, an, dsk, nw, s0_t, c0, y_prev, *, row_off, nseq, nwalk, L, Lpad, C):
    nc = Lpad // C
    base = row_off // C

    def rmap(b, c):
        return (base + b * nc + c, 0)

    const = lambda b, c: (0, 0)
    in_specs = [
        pl.BlockSpec((C, SSD_INNER), rmap),
        pl.BlockSpec((C, SSD_CONV_DIM), rmap),
        pl.BlockSpec((C, LANES), rmap),
        pl.BlockSpec((SSD_CONV_W, SSD_CONV_DIM), const),
        pl.BlockSpec((1, SSD_CONV_DIM), const),
        pl.BlockSpec((1, LANES), const),
        pl.BlockSpec((1, LANES), const),
        pl.BlockSpec((1, SSD_INNER), const),
        pl.BlockSpec((1, SSD_INNER), const),
        pl.BlockSpec((1, SSD_STATE, SSD_INNER), lambda b, c: (jnp.minimum(b, nseq - 1), 0, 0)),
        pl.BlockSpec((1, 8, SSD_CONV_DIM), lambda b, c: (jnp.minimum(b, nseq - 1), 0, 0)),
        pl.BlockSpec(memory_space=pl.ANY),
    ]
    out_specs = [
        pl.BlockSpec((C, SSD_INNER), rmap),
        pl.BlockSpec((1, SSD_STATE, SSD_INNER), lambda b, c: (b, 0, 0)),
    ]
    out_shape = [
        jax.ShapeDtypeStruct(y_prev.shape, y_prev.dtype),
        jax.ShapeDtypeStruct((nwalk, SSD_STATE, SSD_INNER), F32),
    ]
    scratch = [
        pltpu.VMEM((SSD_STATE, SSD_INNER), F32),
        pltpu.VMEM((C + 8, SSD_CONV_DIM), F32),
        pltpu.VMEM((C, SSD_CONV_DIM), F32),
    ]
    y, st = pl.pallas_call(
        functools.partial(_ssd_kernel, C=C, L=L), grid=(nwalk, nc), in_specs=in_specs, out_specs=out_specs,
        out_shape=out_shape, scratch_shapes=scratch, input_output_aliases={11: 0},
        compiler_params=_params(("arbitrary", "arbitrary")))(z, xbc, dt, cw, cb, dtb, an, dsk, nw, s0_t, c0, y_prev)
    return y, st[:nseq]


def _gla_kernel(q_ref, k_ref, v_ref, g_ref, lr_ref, w2_ref, b2_ref, nw_ref, s0_ref,
                oprev_ref, o_ref, sout_ref, st_ref, *, C, L, nsub):
    del oprev_ref
    c = pl.program_id(1)
    sub = GLA_SUB
    nb = C // sub

    @pl.when(c == 0)
    def _():
        st_ref[...] = s0_ref[0]

    ti = _iota((C, C), 0)
    si = _iota((C, C), 1)
    tril = si <= ti
    tril_b = tril.astype(BF16)
    diag_mask = jnp.logical_and(tril, (ti // sub) == (si // sub))
    lane_o = si % sub
    scale = GLA_DK ** -0.5

    intra = {}
    for s in range(nsub):
        rs = slice(s * C, (s + 1) * C)
        first_row = (c * nsub + s) * C
        gk = jnp.dot(lr_ref[rs, :].astype(BF16), w2_ref[...], preferred_element_type=F32) + b2_ref[...]
        logg = (jnp.minimum(gk, 0.0) - jnp.log1p(jnp.exp(-jnp.abs(gk)))) * (1.0 / GLA_GATE_NORM)
        logg = jnp.where(_iota((C, GLA_KEY_DIM), 0) + first_row < L, logg, 0.0)
        cum_all = _exact_dot(tril_b, logg, NN)
        valid_v = _iota((C, GLA_DV), 0) + first_row < L
        for h in range(GLA_HEADS):
            ks = slice(h * GLA_DK, (h + 1) * GLA_DK)
            vs = slice(h * GLA_DV, (h + 1) * GLA_DV)
            q = q_ref[rs, ks].astype(F32) * scale
            k = k_ref[rs, ks].astype(F32)
            v = jnp.where(valid_v, v_ref[rs, vs].astype(F32), 0.0)
            v_b = v.astype(BF16)
            cum = cum_all[:, ks]
            if nb > 1:
                q_parts, k_parts = [], []
                for i in range(1, nb):
                    r_i = cum[i * sub - 1:i * sub, :]
                    qi = q[i * sub:(i + 1) * sub, :] * jnp.exp(cum[i * sub:(i + 1) * sub, :] - r_i)
                    pieces = [jnp.zeros((i * sub, GLA_DK), F32), qi]
                    if C - (i + 1) * sub > 0:
                        pieces.append(jnp.zeros((C - (i + 1) * sub, GLA_DK), F32))
                    q_parts.append(jnp.concatenate(pieces, axis=0))
                    ki = k[0:i * sub, :] * jnp.exp(r_i - cum[0:i * sub, :])
                    k_parts.append(jnp.concatenate([ki, jnp.zeros((C - i * sub, GLA_DK), F32)], axis=0))
                q_st = jnp.concatenate(q_parts, axis=1).astype(BF16)
                k_st = jnp.concatenate(k_parts, axis=1).astype(BF16)
                a_off = lax.dot_general(q_st, k_st, NT, preferred_element_type=F32)
            else:
                a_off = jnp.zeros((C, C), F32)
            q3 = q.reshape(nb, sub, GLA_DK)
            k3 = k.reshape(nb, sub, GLA_DK)
            c3 = cum.reshape(nb, sub, GLA_DK)
            a_diag = jnp.zeros((C, C), F32)
            for o in range(sub):
                p_o = q3 * jnp.exp(jnp.minimum(c3 - c3[:, o:o + 1, :], 0.0)) * k3[:, o:o + 1, :]
                a_diag = jnp.where(lane_o == o, jnp.sum(p_o.reshape(C, GLA_DK), axis=-1, keepdims=True), a_diag)
            a = a_off + jnp.where(diag_mask, a_diag, 0.0)
            y_intra = jnp.dot(a.astype(BF16), v_b, preferred_element_type=F32)
            last = cum[C - 1:C, :]
            q_e = (q * jnp.exp(cum)).astype(BF16)
            k_e = (k * jnp.exp(last - cum)).astype(BF16)
            intra[s, h] = (y_intra, q_e, k_e, v_b, jnp.exp(last))

    for h in range(GLA_HEADS):
        vs = slice(h * GLA_DV, (h + 1) * GLA_DV)
        st = st_ref[vs, :]
        for s in range(nsub):
            rs = slice(s * C, (s + 1) * C)
            y_intra, q_e, k_e, v_b, e_last = intra[s, h]
            y = y_intra + lax.dot_general(q_e, st.astype(BF16), NT, preferred_element_type=F32)
            st = st * e_last + lax.dot_general(v_b, k_e, TN, preferred_element_type=F32)
            ms = jnp.mean(y * y, axis=-1, keepdims=True)
            gate = g_ref[rs, vs].astype(F32)
            o_ref[rs, vs] = (y * lax.rsqrt(ms + NORM_EPS) * nw_ref[...]
                             * (gate * _sigmoid(gate))).astype(o_ref.dtype)
        st_ref[vs, :] = st

    @pl.when(c == pl.num_programs(1) - 1)
    def _():
        sout_ref[0] = st_ref[...]


def _gla_scan(qkvg, lr, w2, b2, nw, s0_t, o_prev, *, row_off, nseq, nwalk, L, Lpad, C, nsub):
    blk = nsub * C
    nc = Lpad // blk
    base = row_off // blk
    const = lambda b, c: (0, 0)
    in_specs = [
        pl.BlockSpec((blk, GLA_KEY_DIM), lambda b, c: (base + b * nc + c, 0)),
        pl.BlockSpec((blk, GLA_KEY_DIM), lambda b, c: (base + b * nc + c, 1)),
        pl.BlockSpec((blk, GLA_VAL_DIM), lambda b, c: (base + b * nc + c, 1)),
        pl.BlockSpec((blk, GLA_VAL_DIM), lambda b, c: (base + b * nc + c, 2)),
        pl.BlockSpec((blk, LANES), lambda b, c: (base + b * nc + c, 0)),
        pl.BlockSpec((LANES, GLA_KEY_DIM), const),
        pl.BlockSpec((1, GLA_KEY_DIM), const),
        pl.BlockSpec((1, GLA_DV), const),
        pl.BlockSpec((1, GLA_VAL_DIM, GLA_DK), lambda b, c: (jnp.minimum(b, nseq - 1), 0, 0)),
        pl.BlockSpec(memory_space=pl.ANY),
    ]
    out_specs = [
        pl.BlockSpec((blk, GLA_VAL_DIM), lambda b, c: (base + b * nc + c, 0)),
        pl.BlockSpec((1, GLA_VAL_DIM, GLA_DK), lambda b, c: (b, 0, 0)),
    ]
    out_shape = [
        jax.ShapeDtypeStruct(o_prev.shape, o_prev.dtype),
        jax.ShapeDtypeStruct((nwalk, GLA_VAL_DIM, GLA_DK), F32),
    ]
    o, st = pl.pallas_call(
        functools.partial(_gla_kernel, C=C, L=L, nsub=nsub), grid=(nwalk, nc), in_specs=in_specs, out_specs=out_specs,
        out_shape=out_shape, scratch_shapes=[pltpu.VMEM((GLA_VAL_DIM, GLA_DK), F32)],
        input_output_aliases={9: 0},
        compiler_params=_params(("arbitrary", "arbitrary")))(qkvg, qkvg, qkvg, qkvg, lr, w2, b2, nw, s0_t, o_prev)
    return o, st[:nseq]


def _layer_norm(x, g, b):
    mu = jnp.mean(x, axis=-1, keepdims=True)
    xc = x - mu
    var = jnp.mean(xc * xc, axis=-1, keepdims=True)
    return xc * lax.rsqrt(var + NORM_EPS) * g + b


def _post_kernel(y_ref, wo_ref, h_ref, g_ref, b_ref, wr_ref, br_ref,
                 h1_ref, h1b_ref, ri_ref, rw_ref, cnt_ref, carry_ref, *, alpha):
    i = pl.program_id(0)
    tm = h_ref.shape[0]

    @pl.when(i == 0)
    def _():
        carry_ref[...] = jnp.zeros_like(carry_ref)

    mix = jnp.dot(y_ref[...], wo_ref[...], preferred_element_type=F32)
    h1 = _layer_norm(alpha * h_ref[...] + mix, g_ref[...], b_ref[...])
    h1_ref[...] = h1
    h_hi = h1.astype(BF16)
    h1b_ref[...] = h_hi
    logits = jnp.dot(h_hi, wr_ref[...], preferred_element_type=F32) + br_ref[...]
    lane = _iota((tm, LANES), 1)
    neg = -jnp.inf
    is_g = lane < MOE_GROUPS
    gl = jnp.where(is_g, logits, neg)
    gmax = jnp.max(gl, axis=-1, keepdims=True)
    gsel = jnp.min(jnp.where(gl == gmax, lane, LANES), axis=-1, keepdims=True)
    p_g = 1.0 / jnp.sum(jnp.where(is_g, jnp.exp(gl - gmax), 0.0), axis=-1, keepdims=True)
    eid = lane - MOE_GROUPS
    in_g = (eid >= 0) & (eid < MOE_EXPERTS) & ((eid // MOE_EXP_PER_GROUP) == gsel)
    el = jnp.where(in_g, logits, neg)
    v1 = jnp.max(el, axis=-1, keepdims=True)
    i1 = jnp.min(jnp.where(el == v1, lane, LANES), axis=-1, keepdims=True)
    el2 = jnp.where(lane == i1, neg, el)
    v2 = jnp.max(el2, axis=-1, keepdims=True)
    i2 = jnp.min(jnp.where(el2 == v2, lane, LANES), axis=-1, keepdims=True)
    t = jnp.exp(v2 - v1)
    w1 = p_g / (1.0 + t)
    w2 = p_g * t / (1.0 + t)
    e1 = i1 - MOE_GROUPS
    e2 = i2 - MOE_GROUPS
    oh = jnp.where(lane == e1, 1.0, 0.0) + jnp.where(lane == e2, 1.0, 0.0)
    strict = (_iota((tm, tm), 1) < _iota((tm, tm), 0)).astype(BF16)
    before = jnp.dot(strict, oh.astype(BF16), preferred_element_type=F32) + carry_ref[...]
    r1 = jnp.sum(jnp.where(lane == e1, before, 0.0), axis=-1, keepdims=True)
    r2 = jnp.sum(jnp.where(lane == e2, before, 0.0), axis=-1, keepdims=True)
    carry_ref[...] = carry_ref[...] + jnp.sum(oh, axis=0, keepdims=True)
    cnt_ref[...] = carry_ref[...]
    packed = jnp.where(lane == 0, e1.astype(F32), jnp.where(lane == 1, e2.astype(F32), jnp.where(
        lane == 2, r1, jnp.where(lane == 3, r2, 0.0))))
    ri_ref[...] = packed.T[0:8, :]
    rw_ref[...] = jnp.where(lane == 0, w1, jnp.where(lane == 1, w2, 0.0))


def _post_mixer(y, wo, h, g, b, wr, br, *, alpha):
    rows, kin = y.shape
    grid = (rows // POST_TILE,)
    rmap = lambda i: (i, 0)
    const = lambda i: (0, 0)
    in_specs = [
        pl.BlockSpec((POST_TILE, kin), rmap),
        pl.BlockSpec((kin, D_MODEL), const),
        pl.BlockSpec((POST_TILE, D_MODEL), rmap),
        pl.BlockSpec((1, D_MODEL), const),
        pl.BlockSpec((1, D_MODEL), const),
        pl.BlockSpec((D_MODEL, LANES), const),
        pl.BlockSpec((1, LANES), const),
    ]
    out_specs = [
        pl.BlockSpec((POST_TILE, D_MODEL), rmap),
        pl.BlockSpec((POST_TILE, D_MODEL), rmap),
        pl.BlockSpec((8, POST_TILE), lambda i: (0, i)),
        pl.BlockSpec((POST_TILE, LANES), rmap),
        pl.BlockSpec((1, LANES), const),
    ]
    out_shape = [
        jax.ShapeDtypeStruct((rows, D_MODEL), F32),
        jax.ShapeDtypeStruct((rows, D_MODEL), BF16),
        jax.ShapeDtypeStruct((8, rows), F32),
        jax.ShapeDtypeStruct((rows, LANES), F32),
        jax.ShapeDtypeStruct((1, LANES), F32),
    ]
    return pl.pallas_call(
        functools.partial(_post_kernel, alpha=alpha), grid=grid, in_specs=in_specs, out_specs=out_specs,
        out_shape=out_shape, scratch_shapes=[pltpu.VMEM((1, LANES), F32)],
        compiler_params=_params(("arbitrary",)))(y, wo, h, g, b, wr, br)


def _expert_kernel(be_ref, nu_ref, x_ref, wg_ref, wu_ref, wd_ref, o_ref, wg_b, wu_b, wd_b):
    i = pl.program_id(0)
    prev = be_ref[jnp.maximum(i - 1, 0)]

    @pl.when(jnp.logical_or(i == 0, be_ref[i] != prev))
    def _():
        wg_b[...] = wg_ref[...].astype(BF16)
        wu_b[...] = wu_ref[...].astype(BF16)
        wd_b[...] = wd_ref[...].astype(BF16)

    @pl.when(i < nu_ref[0])
    def _():
        x = x_ref[...]
        hg = jnp.dot(x, wg_b[...], preferred_element_type=F32)
        hu = jnp.dot(x, wu_b[...], preferred_element_type=F32)
        hh = (hg * _sigmoid(hg) * hu).astype(BF16)
        o_ref[...] = jnp.dot(hh, wd_b[...], preferred_element_type=F32).astype(o_ref.dtype)

    @pl.when(i >= nu_ref[0])
    def _():
        o_ref[...] = jnp.zeros_like(o_ref)


def _experts(block_e, n_used, xb, w_gate, w_up, w_down, layer):
    nblk = xb.shape[0] // EXPERT_BLOCK
    wmap = lambda i, be, nu: (layer, be[i], 0, 0)
    grid_spec = pltpu.PrefetchScalarGridSpec(
        num_scalar_prefetch=2, grid=(nblk,),
        in_specs=[
            pl.BlockSpec((EXPERT_BLOCK, D_MODEL), lambda i, be, nu: (i, 0)),
            pl.BlockSpec((None, None, D_MODEL, EXPERT_FF), wmap),
            pl.BlockSpec((None, None, D_MODEL, EXPERT_FF), wmap),
            pl.BlockSpec((None, None, EXPERT_FF, D_MODEL), wmap),
        ],
        out_specs=pl.BlockSpec((EXPERT_BLOCK, D_MODEL), lambda i, be, nu: (i, 0)),
        scratch_shapes=[pltpu.VMEM((D_MODEL, EXPERT_FF), BF16), pltpu.VMEM((D_MODEL, EXPERT_FF), BF16),
                        pltpu.VMEM((EXPERT_FF, D_MODEL), BF16)])
    return pl.pallas_call(
        _expert_kernel, grid_spec=grid_spec,
        out_shape=jax.ShapeDtypeStruct((nblk * EXPERT_BLOCK, D_MODEL), BF16),
        compiler_params=_params(("arbitrary",)))(block_e, n_used, xb, w_gate, w_up, w_down)


def _combine_kernel(h_ref, ya_ref, yb_ref, rw_ref, g_ref, b_ref, o_ref, *, alpha):
    rw = rw_ref[...]
    ffn = ya_ref[...].astype(F32) * rw[:, 0:1] + yb_ref[...].astype(F32) * rw[:, 1:2]
    o_ref[...] = _layer_norm(alpha * h_ref[...] + ffn, g_ref[...], b_ref[...])


def _combine(h1, ya, yb, rw, g, b, *, alpha):
    rows = h1.shape[0]
    rmap = lambda i: (i, 0)
    const = lambda i: (0, 0)
    in_specs = [
        pl.BlockSpec((POST_TILE, D_MODEL), rmap), pl.BlockSpec((POST_TILE, D_MODEL), rmap),
        pl.BlockSpec((POST_TILE, D_MODEL), rmap), pl.BlockSpec((POST_TILE, LANES), rmap),
        pl.BlockSpec((1, D_MODEL), const), pl.BlockSpec((1, D_MODEL), const),
    ]
    return pl.pallas_call(
        functools.partial(_combine_kernel, alpha=alpha), grid=(rows // POST_TILE,), in_specs=in_specs,
        out_specs=pl.BlockSpec((POST_TILE, D_MODEL), rmap),
        out_shape=jax.ShapeDtypeStruct((rows, D_MODEL), F32),
        compiler_params=_params(("parallel",)))(h1, ya, yb, rw, g, b)


def _round_up(x, m):
    return (x + m - 1) // m * m


def _pad_cols(w, n):
    return jnp.pad(w, ((0, 0), (0, n - w.shape[1])))


def _take_rows(x, idx):
    return x.at[idx].get(mode='promise_in_bounds')


def _moe(h1b, ri, cnt, w_gate, w_up, w_down, layer):
    rows = h1b.shape[0]
    nblk = (2 * rows + MOE_EXPERTS * (EXPERT_BLOCK - 1) + EXPERT_BLOCK - 1) // EXPERT_BLOCK
    counts = cnt[0, :MOE_EXPERTS].astype(I32)
    pcounts = (counts + EXPERT_BLOCK - 1) // EXPERT_BLOCK * EXPERT_BLOCK
    pends = jnp.cumsum(pcounts)
    pstarts = pends - pcounts
    ri = ri.astype(I32)
    dest = _take_rows(pstarts, ri[0:2].reshape(-1)).reshape(2, rows) + ri[2:4]
    tok = jnp.broadcast_to(jnp.arange(rows, dtype=I32)[None, :], (2, rows))
    row_tok = (jnp.arange(nblk * EXPERT_BLOCK, dtype=I32) % rows).at[dest.reshape(-1)].set(
        tok.reshape(-1), mode='promise_in_bounds', unique_indices=True)
    blk_start = jnp.arange(nblk, dtype=I32) * EXPERT_BLOCK
    block_e = jnp.minimum(jnp.sum((pends[None, :] <= blk_start[:, None]).astype(I32), axis=1), MOE_EXPERTS - 1)
    n_used = (pends[-1:] // EXPERT_BLOCK).astype(I32)
    xb = _take_rows(h1b, row_tok)
    yb = _experts(block_e, n_used, xb, w_gate, w_up, w_down, layer)
    return _take_rows(yb, dest[0]), _take_rows(yb, dest[1])


def kernel(x_prompt, x_sample, state_ssd, state_ssd_conv, state_gla, meta_tokens, ssd_w_in, ssd_conv_w, ssd_conv_b, ssd_dt_bias, ssd_A_log, ssd_D, ssd_norm_w, ssd_w_out, gla_w_in, gla_w_gk2, gla_b_gk2, gla_norm_w, gla_w_out, ln1_g, ln1_b, moe_w_grp, moe_b_grp, moe_w_exp, moe_b_exp, moe_w_gate, moe_w_up, moe_w_down, ln2_g, ln2_b):
    bp, seq, _ = x_prompt.shape
    bs, lsm, _ = x_sample.shape
    depth = ln1_g.shape[0]
    alpha = (2.0 * depth) ** 0.25
    lp = N_META + seq
    cp = PROMPT_CHUNK
    lp_pad = _round_up(lp, PROMPT_STEP_CHUNKS * cp)
    cs = lsm
    off_s = bp * lp_pad
    assert off_s % cs == 0 and cs % GLA_SUB == 0 and cp % GLA_SUB == 0
    rows = _round_up(off_s + bs * lsm, POST_TILE)

    pieces = []
    for b in range(bp):
        pieces += [meta_tokens.astype(F32), x_prompt[b], jnp.zeros((lp_pad - lp, D_MODEL), F32)]
    pieces += [x_sample.reshape(bs * lsm, D_MODEL), jnp.zeros((rows - off_s - bs * lsm, D_MODEL), F32)]
    h = jnp.concatenate(pieces, axis=0)

    groups = (dict(row_off=0, nseq=bp, nwalk=bp, L=lp, Lpad=lp_pad, C=cp),
              dict(row_off=off_s, nseq=bs, nwalk=(rows - off_s) // cs, L=lsm, Lpad=lsm, C=cs))
    ymix_ssd = jnp.zeros((rows, SSD_INNER), BF16)
    ymix_gla = jnp.zeros((rows, GLA_VAL_DIM), BF16)

    new_ssd_p, new_ssd_s, new_conv_p, new_conv_s, new_gla_p, new_gla_s = [], [], [], [], [], []
    for i in range(depth):
        j = i // 2
        if i % 2 == 0:
            w_in = ssd_w_in[j].astype(BF16)
            z, xbc, dtr = _proj(
                h, [w_in[:, :SSD_INNER], w_in[:, SSD_INNER:SSD_INNER + SSD_CONV_DIM],
                    _pad_cols(w_in[:, SSD_INNER + SSD_CONV_DIM:], LANES)], [F32, F32, F32])
            cw = ssd_conv_w[j]
            cb = ssd_conv_b[j][None]
            dtb = _pad_cols(ssd_dt_bias[j][None], LANES)
            an = _pad_cols(-jnp.exp(ssd_A_log[j].astype(F32))[None], LANES)
            dsk = jnp.repeat(ssd_D[j], SSD_HEAD_DIM)[None]
            nw = ssd_norm_w[j][None]
            s0s = [jnp.zeros((bp, SSD_STATE, SSD_INNER), F32),
                   jnp.swapaxes(state_ssd[j].reshape(bs, SSD_INNER, SSD_STATE), 1, 2)]
            c0s = [jnp.zeros((bp, 8, SSD_CONV_DIM), F32),
                   jnp.pad(state_ssd_conv[j], ((0, 0), (8 - (SSD_CONV_W - 1), 0), (0, 0)))]
            for grp, s0, c0, acc_s, acc_c in zip(groups, s0s, c0s, (new_ssd_p, new_ssd_s), (new_conv_p, new_conv_s)):
                ymix_ssd, st = _ssd_scan(z, xbc, dtr, cw, cb, dtb, an, dsk, nw, s0, c0, ymix_ssd, **grp)
                n, L, Lpad = grp['nseq'], grp['L'], grp['Lpad']
                acc_s.append(jnp.swapaxes(st, 1, 2).reshape(n, SSD_HEADS, SSD_HEAD_DIM, SSD_STATE))
                nconv = SSD_CONV_W - 1
                last = (grp['row_off'] + jnp.arange(n, dtype=I32)[:, None] * Lpad + (L - nconv)
                        + jnp.arange(nconv, dtype=I32)[None, :])
                acc_c.append(_take_rows(xbc, last.reshape(-1)).reshape(n, nconv, SSD_CONV_DIM))
            ymix = ymix_ssd
            wo = ssd_w_out[j].astype(BF16)
        else:
            w_in = gla_w_in[j].astype(BF16)
            nq = 2 * GLA_KEY_DIM + 2 * GLA_VAL_DIM
            qkvg, lr = _proj(h, [w_in[:, :nq], _pad_cols(w_in[:, nq:], LANES)], [F32, F32])
            w2 = jnp.pad(gla_w_gk2[j], ((0, LANES - GLA_GATE_RANK), (0, 0))).astype(BF16)
            b2 = gla_b_gk2[j][None]
            nw = gla_norm_w[j][None]
            s0s = [jnp.zeros((bp, GLA_VAL_DIM, GLA_DK), F32),
                   jnp.swapaxes(state_gla[j], 2, 3).reshape(bs, GLA_VAL_DIM, GLA_DK)]
            for grp, s0, acc, nsub in zip(groups, s0s, (new_gla_p, new_gla_s), (PROMPT_STEP_CHUNKS, 1)):
                ymix_gla, st = _gla_scan(qkvg, lr, w2, b2, nw, s0, ymix_gla, nsub=nsub, **grp)
                acc.append(jnp.swapaxes(st.reshape(grp['nseq'], GLA_HEADS, GLA_DV, GLA_DK), 2, 3))
            ymix = ymix_gla
            wo = gla_w_out[j].astype(BF16)

        wr = _pad_cols(jnp.concatenate([moe_w_grp[i], moe_w_exp[i]], axis=1), LANES).astype(BF16)
        br = _pad_cols(jnp.concatenate([moe_b_grp[i], moe_b_exp[i]])[None], LANES)
        h1, h1b, ri, rw, cnt = _post_mixer(ymix, wo, h, ln1_g[i][None], ln1_b[i][None], wr, br, alpha=alpha)
        ya, yb = _moe(h1b, ri, cnt, moe_w_gate, moe_w_up, moe_w_down, i)
        h = _combine(h1, ya, yb, rw, ln2_g[i][None], ln2_b[i][None], alpha=alpha)

    y_prompt = h[:off_s].reshape(bp, lp_pad, D_MODEL)[:, N_META:lp]
    y_sample = h[off_s:off_s + bs * lsm].reshape(bs, lsm, D_MODEL)
    return (y_prompt, y_sample, jnp.stack(new_ssd_p), jnp.stack(new_conv_p), jnp.stack(new_gla_p),
            jnp.stack(new_ssd_s), jnp.stack(new_conv_s), jnp.stack(new_gla_s))
```

```python
import functools
import math

import jax
import jax.numpy as jnp
from jax import lax
from jax.experimental import pallas as pl
from jax.experimental.pallas import tpu as pltpu

F32 = jnp.float32
BF16 = jnp.bfloat16
I32 = jnp.int32
NT = (((1,), (1,)), ((), ()))
TN = (((0,), (0,)), ((), ()))
NN = (((1,), (0,)), ((), ()))

D_MODEL = 1024
N_META = 16
SSD_INNER = 2048
SSD_HEAD_DIM = 64
SSD_HEADS = 32
SSD_GROUPS = 4
SSD_STATE = 128
SSD_CONV_W = 4
SSD_CONV_DIM = 3072
GLA_HEADS = 4
GLA_KEY_DIM = 512
GLA_VAL_DIM = 1024
GLA_DK = 128
GLA_DV = 256
GLA_GATE_RANK = 16
GLA_GATE_NORM = 16.0
MOE_GROUPS = 4
MOE_EXP_PER_GROUP = 8
MOE_EXPERTS = 32
EXPERT_FF = 512
NORM_EPS = 1e-5

LANES = 128
VMEM_LIMIT_MB = 56
ROW_TILE = 256
POST_TILE = 512
PROMPT_CHUNK = 128
PROMPT_STEP_CHUNKS = 2
GLA_SUB = 8
EXPERT_BLOCK = 256
INV_CHUNK = 1024


def _params(sem):
    return pltpu.CompilerParams(dimension_semantics=sem, vmem_limit_bytes=VMEM_LIMIT_MB << 20)


def _sigmoid(x):
    return 1.0 / (1.0 + jnp.exp(-x))


def _softplus(x):
    return jnp.maximum(x, 0.0) + jnp.log1p(jnp.exp(-jnp.abs(x)))


def _iota(shape, dim):
    return lax.broadcasted_iota(I32, shape, dim)


def _exact_dot(a_b, x, dims):
    hi = x.astype(BF16)
    r1 = x - hi.astype(F32)
    mid = r1.astype(BF16)
    lo = (r1 - mid.astype(F32)).astype(BF16)
    f = lambda t: lax.dot_general(a_b, t, dims, preferred_element_type=F32)
    return (f(hi) + f(mid)) + f(lo)


def _proj_kernel(x_ref, *refs):
    n = len(refs) // 2
    xb = x_ref[...].astype(BF16)
    for w_ref, o_ref in zip(refs[:n], refs[n:]):
        o_ref[...] = jnp.dot(xb, w_ref[...], preferred_element_type=F32).astype(o_ref.dtype)


def _proj(x, ws, out_dtypes):
    rows, k = x.shape
    grid = (rows // ROW_TILE,)
    in_specs = [pl.BlockSpec((ROW_TILE, k), lambda i: (i, 0))]
    in_specs += [pl.BlockSpec(w.shape, lambda i: (0, 0)) for w in ws]
    out_specs = [pl.BlockSpec((ROW_TILE, w.shape[1]), lambda i: (i, 0)) for w in ws]
    out_shape = [jax.ShapeDtypeStruct((rows, w.shape[1]), dt) for w, dt in zip(ws, out_dtypes)]
    return pl.pallas_call(
        _proj_kernel, grid=grid, in_specs=in_specs, out_specs=out_specs, out_shape=out_shape,
        compiler_params=_params(("parallel",)))(x, *ws)


def _ssd_kernel(z_ref, xbc_ref, dt_ref, cw_ref, cb_ref, dtb_ref, an_ref, dsk_ref, nw_ref, s0_ref, c0_ref,
                yprev_ref, y_ref, sout_ref, st_ref, xc_ref, u_ref, *, C, L):
    del yprev_ref
    c = pl.program_id(1)
    hist = 8

    @pl.when(c == 0)
    def _():
        st_ref[...] = s0_ref[0]
        xc_ref[0:hist, :] = c0_ref[0]

    @pl.when(c > 0)
    def _():
        xc_ref[0:hist, :] = xc_ref[C:C + hist, :]

    xc_ref[hist:hist + C, :] = xbc_ref[...].astype(F32)

    for j in range(0, SSD_CONV_DIM, 512):
        sl = slice(j, j + 512)
        acc = cb_ref[:, sl] + xc_ref[hist:hist + C, sl] * cw_ref[3:4, sl]
        for w in range(SSD_CONV_W - 1):
            off = hist - (SSD_CONV_W - 1) + w
            acc = acc + xc_ref[off:off + C, sl] * cw_ref[w:w + 1, sl]
        u_ref[:, sl] = acc * _sigmoid(acc)

    row = _iota((C, LANES), 0) + c * C
    dt = jnp.where(row < L, _softplus(dt_ref[...] + dtb_ref[...]), 0.0)
    loga = dt * an_ref[...]
    ti = _iota((C, C), 0)
    si = _iota((C, C), 1)
    tril = si <= ti
    eye_b = (_iota((LANES, LANES), 0) == _iota((LANES, LANES), 1)).astype(BF16)
    cum = _exact_dot(tril.astype(BF16), loga, NN)
    cum_t = _exact_dot(eye_b, cum, NT)
    dt_t = _exact_dot(eye_b, dt, NT)
    w_t = dt_t * jnp.exp(cum_t[:, C - 1:C] - cum_t)
    src_t = cum_t - jnp.log(dt_t)
    e_last = jnp.exp(cum[C - 1:C, :])
    lane = _iota((C, LANES), 1)
    lane_s = _iota((SSD_STATE, LANES), 1)
    lane_r = _iota((1, LANES), 1)
    hpg = SSD_HEADS // SSD_GROUPS

    for g in range(SSD_GROUPS):
        b_f = u_ref[:, SSD_INNER + g * SSD_STATE:SSD_INNER + (g + 1) * SSD_STATE]
        c_f = u_ref[:, SSD_INNER + (SSD_GROUPS + g) * SSD_STATE:SSD_INNER + (SSD_GROUPS + g + 1) * SSD_STATE]
        b_b = b_f.astype(BF16)
        cb = lax.dot_general(c_f.astype(BF16), b_b, NT, preferred_element_type=F32)
        b_t = lax.dot_general(eye_b, b_b, NT, preferred_element_type=F32)
        ypairs = []
        for jp in range(hpg // 2):
            p = g * (hpg // 2) + jp
            cs = slice(p * LANES, (p + 1) * LANES)
            xs_p = u_ref[:, cs]
            xs_b = xs_p.astype(BF16)
            st_p = st_ref[:, cs]
            st_b = st_p.astype(BF16)
            res, upd = [], []
            for h in (2 * p, 2 * p + 1):
                ccol = cum[:, h:h + 1]
                dec = jnp.where(tril, jnp.exp(ccol - src_t[h:h + 1, :]), 0.0)
                wmat = (cb * dec).astype(BF16)
                c_e = (c_f * jnp.exp(ccol)).astype(BF16)
                res.append(jnp.dot(wmat, xs_b, preferred_element_type=F32)
                           + jnp.dot(c_e, st_b, preferred_element_type=F32))
                b_w = (b_t * w_t[h:h + 1, :]).astype(BF16)
                upd.append(jnp.dot(b_w, xs_b, preferred_element_type=F32))
            y_p = jnp.where(lane < SSD_HEAD_DIM, res[0], res[1]) + dsk_ref[:, cs] * xs_p
            e_p = jnp.where(lane_r < SSD_HEAD_DIM, e_last[:, 2 * p:2 * p + 1], e_last[:, 2 * p + 1:2 * p + 2])
            st_ref[:, cs] = st_p * e_p + jnp.where(lane_s < SSD_HEAD_DIM, upd[0], upd[1])
            ypairs.append(y_p)
        gs = slice(g * 512, (g + 1) * 512)
        yg = jnp.concatenate(ypairs, axis=1)
        zg = z_ref[:, gs].astype(F32)
        yg = yg * (zg * _sigmoid(zg))
        ms = jnp.mean(yg * yg, axis=-1, keepdims=True)
        y_ref[:, gs] = (yg * lax.rsqrt(ms + NORM_EPS) * nw_ref[:, gs]).astype(y_ref.dtype)

    @pl.when(c == pl.num_programs(1) - 1)
    def _():
        sout_ref[0] = st_ref[...]


def _ssd_scan(z, xbc, dt, cw, cb, dtb, an, dsk, nw, s0_t, c0, y_prev, *, row_off, nseq, nwalk, L, Lpad, C):
    nc = Lpad // C
    base = row_off // C

    def rmap(b, c):
        return (base + b * nc + c, 0)

    const = lambda b, c: (0, 0)
    in_specs = [
        pl.BlockSpec((C, SSD_INNER), rmap),
        pl.BlockSpec((C, SSD_CONV_DIM), rmap),
        pl.BlockSpec((C, LANES), rmap),
        pl.BlockSpec((SSD_CONV_W, SSD_CONV_DIM), const),
        pl.BlockSpec((1, SSD_CONV_DIM), const),
        pl.BlockSpec((1, LANES), const),
        pl.BlockSpec((1, LANES), const),
        pl.BlockSpec((1, SSD_INNER), const),
        pl.BlockSpec((1, SSD_INNER), const),
        pl.BlockSpec((1, SSD_STATE, SSD_INNER), lambda b, c: (jnp.minimum(b, nseq - 1), 0, 0)),
        pl.BlockSpec((1, 8, SSD_CONV_DIM), lambda b, c: (jnp.minimum(b, nseq - 1), 0, 0)),
        pl.BlockSpec(memory_space=pl.ANY),
    ]
    out_specs = [
        pl.BlockSpec((C, SSD_INNER), rmap),
        pl.BlockSpec((1, SSD_STATE, SSD_INNER), lambda b, c: (b, 0, 0)),
    ]
    out_shape = [
        jax.ShapeDtypeStruct(y_prev.shape, y_prev.dtype),
        jax.ShapeDtypeStruct((nwalk, SSD_STATE, SSD_INNER), F32),
    ]
    scratch = [
        pltpu.VMEM((SSD_STATE, SSD_INNER), F32),
        pltpu.VMEM((C + 8, SSD_CONV_DIM), F32),
        pltpu.VMEM((C, SSD_CONV_DIM), F32),
    ]
    y, st = pl.pallas_call(
        functools.partial(_ssd_kernel, C=C, L=L), grid=(nwalk, nc), in_specs=in_specs, out_specs=out_specs,
        out_shape=out_shape, scratch_shapes=scratch, input_output_aliases={11: 0},
        compiler_params=_params(("arbitrary", "arbitrary")))(z, xbc, dt, cw, cb, dtb, an, dsk, nw, s0_t, c0, y_prev)
    return y, st[:nseq]


def _gla_kernel(q_ref, k_ref, v_ref, g_ref, lr_ref, w2_ref, b2_ref, nw_ref, s0_ref,
                oprev_ref, o_ref, sout_ref, st_ref, *, C, L, nsub):
    del oprev_ref
    c = pl.program_id(1)
    sub = GLA_SUB
    nb = C // sub

    @pl.when(c == 0)
    def _():
        st_ref[...] = s0_ref[0]

    ti = _iota((C, C), 0)
    si = _iota((C, C), 1)
    tril = si <= ti
    tril_b = tril.astype(BF16)
    diag_mask = jnp.logical_and(tril, (ti // sub) == (si // sub))
    lane_o = si % sub
    scale = GLA_DK ** -0.5

    intra = {}
    for s in range(nsub):
        rs = slice(s * C, (s + 1) * C)
        first_row = (c * nsub + s) * C
        gk = jnp.dot(lr_ref[rs, :].astype(BF16), w2_ref[...], preferred_element_type=F32) + b2_ref[...]
        logg = (jnp.minimum(gk, 0.0) - jnp.log1p(jnp.exp(-jnp.abs(gk)))) * (1.0 / GLA_GATE_NORM)
        logg = jnp.where(_iota((C, GLA_KEY_DIM), 0) + first_row < L, logg, 0.0)
        cum_all = _exact_dot(tril_b, logg, NN)
        valid_v = _iota((C, GLA_DV), 0) + first_row < L
        for h in range(GLA_HEADS):
            ks = slice(h * GLA_DK, (h + 1) * GLA_DK)
            vs = slice(h * GLA_DV, (h + 1) * GLA_DV)
            q = q_ref[rs, ks].astype(F32) * scale
            k = k_ref[rs, ks].astype(F32)
            v = jnp.where(valid_v, v_ref[rs, vs].astype(F32), 0.0)
            v_b = v.astype(BF16)
            cum = cum_all[:, ks]
            if nb > 1:
                q_parts, k_parts = [], []
                for i in range(1, nb):
                    r_i = cum[i * sub - 1:i * sub, :]
                    qi = q[i * sub:(i + 1) * sub, :] * jnp.exp(cum[i * sub:(i + 1) * sub, :] - r_i)
                    pieces = [jnp.zeros((i * sub, GLA_DK), F32), qi]
                    if C - (i + 1) * sub > 0:
                        pieces.append(jnp.zeros((C - (i + 1) * sub, GLA_DK), F32))
                    q_parts.append(jnp.concatenate(pieces, axis=0))
                    ki = k[0:i * sub, :] * jnp.exp(r_i - cum[0:i * sub, :])
                    k_parts.append(jnp.concatenate([ki, jnp.zeros((C - i * sub, GLA_DK), F32)], axis=0))
                q_st = jnp.concatenate(q_parts, axis=1).astype(BF16)
                k_st = jnp.concatenate(k_parts, axis=1).astype(BF16)
                a_off = lax.dot_general(q_st, k_st, NT, preferred_element_type=F32)
            else:
                a_off = jnp.zeros((C, C), F32)
            q3 = q.reshape(nb, sub, GLA_DK)
            k3 = k.reshape(nb, sub, GLA_DK)
            c3 = cum.reshape(nb, sub, GLA_DK)
            a_diag = jnp.zeros((C, C), F32)
            for o in range(sub):
                p_o = q3 * jnp.exp(jnp.minimum(c3 - c3[:, o:o + 1, :], 0.0)) * k3[:, o:o + 1, :]
                a_diag = jnp.where(lane_o == o, jnp.sum(p_o.reshape(C, GLA_DK), axis=-1, keepdims=True), a_diag)
            a = a_off + jnp.where(diag_mask, a_diag, 0.0)
            y_intra = jnp.dot(a.astype(BF16), v_b, preferred_element_type=F32)
            last = cum[C - 1:C, :]
            q_e = (q * jnp.exp(cum)).astype(BF16)
            k_e = (k * jnp.exp(last - cum)).astype(BF16)
            intra[s, h] = (y_intra, q_e, k_e, v_b, jnp.exp(last))

    for h in range(GLA_HEADS):
        vs = slice(h * GLA_DV, (h + 1) * GLA_DV)
        st = st_ref[vs, :]
        for s in range(nsub):
            rs = slice(s * C, (s + 1) * C)
            y_intra, q_e, k_e, v_b, e_last = intra[s, h]
            y = y_intra + lax.dot_general(q_e, st.astype(BF16), NT, preferred_element_type=F32)
            st = st * e_last + lax.dot_general(v_b, k_e, TN, preferred_element_type=F32)
            ms = jnp.mean(y * y, axis=-1, keepdims=True)
            gate = g_ref[rs, vs].astype(F32)
            o_ref[rs, vs] = (y * lax.rsqrt(ms + NORM_EPS) * nw_ref[...]
                             * (gate * _sigmoid(gate))).astype(o_ref.dtype)
        st_ref[vs, :] = st

    @pl.when(c == pl.num_programs(1) - 1)
    def _():
        sout_ref[0] = st_ref[...]


def _gla_scan(qkvg, lr, w2, b2, nw, s0_t, o_prev, *, row_off, nseq, nwalk, L, Lpad, C, nsub):
    blk = nsub * C
    nc = Lpad // blk
    base = row_off // blk
    const = lambda b, c: (0, 0)
    in_specs = [
        pl.BlockSpec((blk, GLA_KEY_DIM), lambda b, c: (base + b * nc + c, 0)),
        pl.BlockSpec((blk, GLA_KEY_DIM), lambda b, c: (base + b * nc + c, 1)),
        pl.BlockSpec((blk, GLA_VAL_DIM), lambda b, c: (base + b * nc + c, 1)),
        pl.BlockSpec((blk, GLA_VAL_DIM), lambda b, c: (base + b * nc + c, 2)),
        pl.BlockSpec((blk, LANES), lambda b, c: (base + b * nc + c, 0)),
        pl.BlockSpec((LANES, GLA_KEY_DIM), const),
        pl.BlockSpec((1, GLA_KEY_DIM), const),
        pl.BlockSpec((1, GLA_DV), const),
        pl.BlockSpec((1, GLA_VAL_DIM, GLA_DK), lambda b, c: (jnp.minimum(b, nseq - 1), 0, 0)),
        pl.BlockSpec(memory_space=pl.ANY),
    ]
    out_specs = [
        pl.BlockSpec((blk, GLA_VAL_DIM), lambda b, c: (base + b * nc + c, 0)),
        pl.BlockSpec((1, GLA_VAL_DIM, GLA_DK), lambda b, c: (b, 0, 0)),
    ]
    out_shape = [
        jax.ShapeDtypeStruct(o_prev.shape, o_prev.dtype),
        jax.ShapeDtypeStruct((nwalk, GLA_VAL_DIM, GLA_DK), F32),
    ]
    o, st = pl.pallas_call(
        functools.partial(_gla_kernel, C=C, L=L, nsub=nsub), grid=(nwalk, nc), in_specs=in_specs, out_specs=out_specs,
        out_shape=out_shape, scratch_shapes=[pltpu.VMEM((GLA_VAL_DIM, GLA_DK), F32)],
        input_output_aliases={9: 0},
        compiler_params=_params(("arbitrary", "arbitrary")))(qkvg, qkvg, qkvg, qkvg, lr, w2, b2, nw, s0_t, o_prev)
    return o, st[:nseq]


def _layer_norm(x, g, b):
    mu = jnp.mean(x, axis=-1, keepdims=True)
    xc = x - mu
    var = jnp.mean(xc * xc, axis=-1, keepdims=True)
    return xc * lax.rsqrt(var + NORM_EPS) * g + b


def _post_kernel(y_ref, wo_ref, h_ref, g_ref, b_ref, wr_ref, br_ref,
                 h1_ref, h1b_ref, ri_ref, rw_ref, cnt_ref, carry_ref, *, alpha):
    i = pl.program_id(0)
    tm = h_ref.shape[0]

    @pl.when(i == 0)
    def _():
        carry_ref[...] = jnp.zeros_like(carry_ref)

    mix = jnp.dot(y_ref[...], wo_ref[...], preferred_element_type=F32)
    h1 = _layer_norm(alpha * h_ref[...] + mix, g_ref[...], b_ref[...])
    h1_ref[...] = h1
    h_hi = h1.astype(BF16)
    h1b_ref[...] = h_hi
    logits = jnp.dot(h_hi, wr_ref[...], preferred_element_type=F32) + br_ref[...]
    lane = _iota((tm, LANES), 1)
    neg = -jnp.inf
    is_g = lane < MOE_GROUPS
    gl = jnp.where(is_g, logits, neg)
    gmax = jnp.max(gl, axis=-1, keepdims=True)
    gsel = jnp.min(jnp.where(gl == gmax, lane, LANES), axis=-1, keepdims=True)
    p_g = 1.0 / jnp.sum(jnp.where(is_g, jnp.exp(gl - gmax), 0.0), axis=-1, keepdims=True)
    eid = lane - MOE_GROUPS
    in_g = (eid >= 0) & (eid < MOE_EXPERTS) & ((eid // MOE_EXP_PER_GROUP) == gsel)
    el = jnp.where(in_g, logits, neg)
    v1 = jnp.max(el, axis=-1, keepdims=True)
    i1 = jnp.min(jnp.where(el == v1, lane, LANES), axis=-1, keepdims=True)
    el2 = jnp.where(lane == i1, neg, el)
    v2 = jnp.max(el2, axis=-1, keepdims=True)
    i2 = jnp.min(jnp.where(el2 == v2, lane, LANES), axis=-1, keepdims=True)
    t = jnp.exp(v2 - v1)
    w1 = p_g / (1.0 + t)
    w2 = p_g * t / (1.0 + t)
    e1 = i1 - MOE_GROUPS
    e2 = i2 - MOE_GROUPS
    oh = jnp.where(lane == e1, 1.0, 0.0) + jnp.where(lane == e2, 1.0, 0.0)
    strict = (_iota((tm, tm), 1) < _iota((tm, tm), 0)).astype(BF16)
    before = jnp.dot(strict, oh.astype(BF16), preferred_element_type=F32) + carry_ref[...]
    r1 = jnp.sum(jnp.where(lane == e1, before, 0.0), axis=-1, keepdims=True)
    r2 = jnp.sum(jnp.where(lane == e2, before, 0.0), axis=-1, keepdims=True)
    carry_ref[...] = carry_ref[...] + jnp.sum(oh, axis=0, keepdims=True)
    cnt_ref[...] = carry_ref[...]
    packed = jnp.where(lane == 0, e1.astype(F32), jnp.where(lane == 1, e2.astype(F32), jnp.where(
        lane == 2, r1, jnp.where(lane == 3, r2, 0.0))))
    ri_ref[...] = packed.T[0:8, :]
    rw_ref[...] = jnp.where(lane == 0, w1, jnp.where(lane == 1, w2, 0.0))


def _post_mixer(y, wo, h, g, b, wr, br, *, alpha):
    rows, kin = y.shape
    grid = (rows // POST_TILE,)
    rmap = lambda i: (i, 0)
    const = lambda i: (0, 0)
    in_specs = [
        pl.BlockSpec((POST_TILE, kin), rmap),
        pl.BlockSpec((kin, D_MODEL), const),
        pl.BlockSpec((POST_TILE, D_MODEL), rmap),
        pl.BlockSpec((1, D_MODEL), const),
        pl.BlockSpec((1, D_MODEL), const),
        pl.BlockSpec((D_MODEL, LANES), const),
        pl.BlockSpec((1, LANES), const),
    ]
    out_specs = [
        pl.BlockSpec((POST_TILE, D_MODEL), rmap),
        pl.BlockSpec((POST_TILE, D_MODEL), rmap),
        pl.BlockSpec((8, POST_TILE), lambda i: (0, i)),
        pl.BlockSpec((POST_TILE, LANES), rmap),
        pl.BlockSpec((1, LANES), const),
    ]
    out_shape = [
        jax.ShapeDtypeStruct((rows, D_MODEL), F32),
        jax.ShapeDtypeStruct((rows, D_MODEL), BF16),
        jax.ShapeDtypeStruct((8, rows), F32),
        jax.ShapeDtypeStruct((rows, LANES), F32),
        jax.ShapeDtypeStruct((1, LANES), F32),
    ]
    return pl.pallas_call(
        functools.partial(_post_kernel, alpha=alpha), grid=grid, in_specs=in_specs, out_specs=out_specs,
        out_shape=out_shape, scratch_shapes=[pltpu.VMEM((1, LANES), F32)],
        compiler_params=_params(("arbitrary",)))(y, wo, h, g, b, wr, br)


def _expert_kernel(be_ref, nu_ref, x_ref, wg_ref, wu_ref, wd_ref, o_ref, wg_b, wu_b, wd_b):
    i = pl.program_id(0)
    prev = be_ref[jnp.maximum(i - 1, 0)]

    @pl.when(jnp.logical_or(i == 0, be_ref[i] != prev))
    def _():
        wg_b[...] = wg_ref[...].astype(BF16)
        wu_b[...] = wu_ref[...].astype(BF16)
        wd_b[...] = wd_ref[...].astype(BF16)

    @pl.when(i < nu_ref[0])
    def _():
        x = x_ref[...]
        hg = jnp.dot(x, wg_b[...], preferred_element_type=F32)
        hu = jnp.dot(x, wu_b[...], preferred_element_type=F32)
        hh = (hg * _sigmoid(hg) * hu).astype(BF16)
        o_ref[...] = jnp.dot(hh, wd_b[...], preferred_element_type=F32).astype(o_ref.dtype)

    @pl.when(i >= nu_ref[0])
    def _():
        o_ref[...] = jnp.zeros_like(o_ref)


def _experts(block_e, n_used, xb, w_gate, w_up, w_down, layer):
    nblk = xb.shape[0] // EXPERT_BLOCK
    wmap = lambda i, be, nu: (layer, be[i], 0, 0)
    grid_spec = pltpu.PrefetchScalarGridSpec(
        num_scalar_prefetch=2, grid=(nblk,),
        in_specs=[
            pl.BlockSpec((EXPERT_BLOCK, D_MODEL), lambda i, be, nu: (i, 0)),
            pl.BlockSpec((None, None, D_MODEL, EXPERT_FF), wmap),
            pl.BlockSpec((None, None, D_MODEL, EXPERT_FF), wmap),
            pl.BlockSpec((None, None, EXPERT_FF, D_MODEL), wmap),
        ],
        out_specs=pl.BlockSpec((EXPERT_BLOCK, D_MODEL), lambda i, be, nu: (i, 0)),
        scratch_shapes=[pltpu.VMEM((D_MODEL, EXPERT_FF), BF16), pltpu.VMEM((D_MODEL, EXPERT_FF), BF16),
                        pltpu.VMEM((EXPERT_FF, D_MODEL), BF16)])
    return pl.pallas_call(
        _expert_kernel, grid_spec=grid_spec,
        out_shape=jax.ShapeDtypeStruct((nblk * EXPERT_BLOCK, D_MODEL), BF16),
        compiler_params=_params(("arbitrary",)))(block_e, n_used, xb, w_gate, w_up, w_down)


def _combine_kernel(h_ref, ya_ref, yb_ref, rw_ref, g_ref, b_ref, o_ref, *, alpha):
    rw = rw_ref[...]
    ffn = ya_ref[...].astype(F32) * rw[:, 0:1] + yb_ref[...].astype(F32) * rw[:, 1:2]
    o_ref[...] = _layer_norm(alpha * h_ref[...] + ffn, g_ref[...], b_ref[...])


def _combine(h1, ya, yb, rw, g, b, *, alpha):
    rows = h1.shape[0]
    rmap = lambda i: (i, 0)
    const = lambda i: (0, 0)
    in_specs = [
        pl.BlockSpec((POST_TILE, D_MODEL), rmap), pl.BlockSpec((POST_TILE, D_MODEL), rmap),
        pl.BlockSpec((POST_TILE, D_MODEL), rmap), pl.BlockSpec((POST_TILE, LANES), rmap),
        pl.BlockSpec((1, D_MODEL), const), pl.BlockSpec((1, D_MODEL), const),
    ]
    return pl.pallas_call(
        functools.partial(_combine_kernel, alpha=alpha), grid=(rows // POST_TILE,), in_specs=in_specs,
        out_specs=pl.BlockSpec((POST_TILE, D_MODEL), rmap),
        out_shape=jax.ShapeDtypeStruct((rows, D_MODEL), F32),
        compiler_params=_params(("parallel",)))(h1, ya, yb, rw, g, b)


def _round_up(x, m):
    return (x + m - 1) // m * m


def _pad_cols(w, n):
    return jnp.pad(w, ((0, 0), (0, n - w.shape[1])))


def _take_rows(x, idx):
    return x.at[idx].get(mode='promise_in_bounds')


def _invert_kernel(npair_ref, dest_ref, init_ref, out_ref, inv_smem, buf0, buf1, sems, *, n_tok):
    nch = dest_ref.shape[0]
    bufs = (buf0, buf1)

    def chunk_copy(ch, slot):
        return pltpu.make_async_copy(dest_ref.at[ch], bufs[slot], sems.at[slot])

    init_copy = pltpu.make_async_copy(init_ref, inv_smem, sems.at[2])
    init_copy.start()
    chunk_copy(0, 0).start()
    init_copy.wait()

    def per_pair(pair, carry):
        for slot in range(2):
            ch = 2 * pair + slot
            chunk_copy(ch, slot).wait()

            @pl.when(ch + 1 < nch)
            def _():
                chunk_copy(ch + 1, 1 - slot).start()

            flat0 = ch * INV_CHUNK
            tok0 = jnp.where(flat0 >= n_tok, flat0 - n_tok, flat0)
            buf = bufs[slot]

            def per_slot(j, inner):
                inv_smem[buf[j]] = tok0 + j
                return inner

            lax.fori_loop(0, INV_CHUNK, per_slot, 0, unroll=16)
        return carry

    lax.fori_loop(0, npair_ref[0], per_pair, 0)
    out_copy = pltpu.make_async_copy(inv_smem, out_ref, sems.at[2])
    out_copy.start()
    out_copy.wait()


def _invert_slots(dest, init):
    n_tok = dest.shape[1]
    assert n_tok % INV_CHUNK == 0 and init.shape[0] % INV_CHUNK == 0
    any_spec = pl.BlockSpec(memory_space=pl.ANY)
    npair = jnp.full((1,), 2 * n_tok // INV_CHUNK // 2, I32)
    grid_spec = pltpu.PrefetchScalarGridSpec(
        num_scalar_prefetch=1, grid=(), in_specs=[any_spec, any_spec], out_specs=any_spec,
        scratch_shapes=[pltpu.SMEM(init.shape, I32), pltpu.SMEM((INV_CHUNK,), I32), pltpu.SMEM((INV_CHUNK,), I32),
                        pltpu.SemaphoreType.DMA((3,))])
    return pl.pallas_call(
        functools.partial(_invert_kernel, n_tok=n_tok), grid_spec=grid_spec,
        out_shape=jax.ShapeDtypeStruct(init.shape, I32),
    )(npair, dest.reshape(-1, INV_CHUNK), init)


def _moe(h1b, ri, cnt, w_gate, w_up, w_down, layer):
    rows = h1b.shape[0]
    nblk = (2 * rows + MOE_EXPERTS * (EXPERT_BLOCK - 1) + EXPERT_BLOCK - 1) // EXPERT_BLOCK
    counts = cnt[0, :MOE_EXPERTS].astype(I32)
    pcounts = (counts + EXPERT_BLOCK - 1) // EXPERT_BLOCK * EXPERT_BLOCK
    pends = jnp.cumsum(pcounts)
    pstarts = pends - pcounts
    ri = ri.astype(I32)
    dest = _take_rows(pstarts, ri[0:2].reshape(-1)).reshape(2, rows) + ri[2:4]
    nslot = _round_up(nblk * EXPERT_BLOCK, INV_CHUNK)
    row_tok = _invert_slots(dest, jnp.arange(nslot, dtype=I32) % rows)[:nblk * EXPERT_BLOCK]
    blk_start = jnp.arange(nblk, dtype=I32) * EXPERT_BLOCK
    block_e = jnp.minimum(jnp.sum((pends[None, :] <= blk_start[:, None]).astype(I32), axis=1), MOE_EXPERTS - 1)
    n_used = (pends[-1:] // EXPERT_BLOCK).astype(I32)
    xb = _take_rows(h1b, row_tok)
    yb = _experts(block_e, n_used, xb, w_gate, w_up, w_down, layer)
    return _take_rows(yb, dest[0]), _take_rows(yb, dest[1])


def kernel(x_prompt, x_sample, state_ssd, state_ssd_conv, state_gla, meta_tokens, ssd_w_in, ssd_conv_w, ssd_conv_b, ssd_dt_bias, ssd_A_log, ssd_D, ssd_norm_w, ssd_w_out, gla_w_in, gla_w_gk2, gla_b_gk2, gla_norm_w, gla_w_out, ln1_g, ln1_b, moe_w_grp, moe_b_grp, moe_w_exp, moe_b_exp, moe_w_gate, moe_w_up, moe_w_down, ln2_g, ln2_b):
    bp, seq, _ = x_prompt.shape
    bs, lsm, _ = x_sample.shape
    depth = ln1_g.shape[0]
    alpha = (2.0 * depth) ** 0.25
    lp = N_META + seq
    cp = PROMPT_CHUNK
    lp_pad = _round_up(lp, PROMPT_STEP_CHUNKS * cp)
    cs = lsm
    off_s = bp * lp_pad
    assert off_s % cs == 0 and cs % GLA_SUB == 0 and cp % GLA_SUB == 0
    rows = _round_up(off_s + bs * lsm, max(POST_TILE, INV_CHUNK))

    pieces = []
    for b in range(bp):
        pieces += [meta_tokens.astype(F32), x_prompt[b], jnp.zeros((lp_pad - lp, D_MODEL), F32)]
    pieces += [x_sample.reshape(bs * lsm, D_MODEL), jnp.zeros((rows - off_s - bs * lsm, D_MODEL), F32)]
    h = jnp.concatenate(pieces, axis=0)

    groups = (dict(row_off=0, nseq=bp, nwalk=bp, L=lp, Lpad=lp_pad, C=cp),
              dict(row_off=off_s, nseq=bs, nwalk=(rows - off_s) // cs, L=lsm, Lpad=lsm, C=cs))
    ymix_ssd = jnp.zeros((rows, SSD_INNER), BF16)
    ymix_gla = jnp.zeros((rows, GLA_VAL_DIM), BF16)

    new_ssd_p, new_ssd_s, new_conv_p, new_conv_s, new_gla_p, new_gla_s = [], [], [], [], [], []
    for i in range(depth):
        j = i // 2
        if i % 2 == 0:
            w_in = ssd_w_in[j].astype(BF16)
            z, xbc, dtr = _proj(
                h, [w_in[:, :SSD_INNER], w_in[:, SSD_INNER:SSD_INNER + SSD_CONV_DIM],
                    _pad_cols(w_in[:, SSD_INNER + SSD_CONV_DIM:], LANES)], [F32, F32, F32])
            cw = ssd_conv_w[j]
            cb = ssd_conv_b[j][None]
            dtb = _pad_cols(ssd_dt_bias[j][None], LANES)
            an = _pad_cols(-jnp.exp(ssd_A_log[j].astype(F32))[None], LANES)
            dsk = jnp.repeat(ssd_D[j], SSD_HEAD_DIM)[None]
            nw = ssd_norm_w[j][None]
            s0s = [jnp.zeros((bp, SSD_STATE, SSD_INNER), F32),
                   jnp.swapaxes(state_ssd[j].reshape(bs, SSD_INNER, SSD_STATE), 1, 2)]
            c0s = [jnp.zeros((bp, 8, SSD_CONV_DIM), F32),
                   jnp.pad(state_ssd_conv[j], ((0, 0), (8 - (SSD_CONV_W - 1), 0), (0, 0)))]
            for grp, s0, c0, acc_s, acc_c in zip(groups, s0s, c0s, (new_ssd_p, new_ssd_s), (new_conv_p, new_conv_s)):
                ymix_ssd, st = _ssd_scan(z, xbc, dtr, cw, cb, dtb, an, dsk, nw, s0, c0, ymix_ssd, **grp)
                n, L, Lpad = grp['nseq'], grp['L'], grp['Lpad']
                acc_s.append(jnp.swapaxes(st, 1, 2).reshape(n, SSD_HEADS, SSD_HEAD_DIM, SSD_STATE))
                nconv = SSD_CONV_W - 1
                last = (grp['row_off'] + jnp.arange(n, dtype=I32)[:, None] * Lpad + (L - nconv)
                        + jnp.arange(nconv, dtype=I32)[None, :])
                acc_c.append(_take_rows(xbc, last.reshape(-1)).reshape(n, nconv, SSD_CONV_DIM))
            ymix = ymix_ssd
            wo = ssd_w_out[j].astype(BF16)
        else:
            w_in = gla_w_in[j].astype(BF16)
            nq = 2 * GLA_KEY_DIM + 2 * GLA_VAL_DIM
            qkvg, lr = _proj(h, [w_in[:, :nq], _pad_cols(w_in[:, nq:], LANES)], [F32, F32])
            w2 = jnp.pad(gla_w_gk2[j], ((0, LANES - GLA_GATE_RANK), (0, 0))).astype(BF16)
            b2 = gla_b_gk2[j][None]
            nw = gla_norm_w[j][None]
            s0s = [jnp.zeros((bp, GLA_VAL_DIM, GLA_DK), F32),
                   jnp.swapaxes(state_gla[j], 2, 3).reshape(bs, GLA_VAL_DIM, GLA_DK)]
            for grp, s0, acc, nsub in zip(groups, s0s, (new_gla_p, new_gla_s), (PROMPT_STEP_CHUNKS, 1)):
                ymix_gla, st = _gla_scan(qkvg, lr, w2, b2, nw, s0, ymix_gla, nsub=nsub, **grp)
                acc.append(jnp.swapaxes(st.reshape(grp['nseq'], GLA_HEADS, GLA_DV, GLA_DK), 2, 3))
            ymix = ymix_gla
            wo = gla_w_out[j].astype(BF16)

        wr = _pad_cols(jnp.concatenate([moe_w_grp[i], moe_w_exp[i]], axis=1), LANES).astype(BF16)
        br = _pad_cols(jnp.concatenate([moe_b_grp[i], moe_b_exp[i]])[None], LANES)
        h1, h1b, ri, rw, cnt = _post_mixer(ymix, wo, h, ln1_g[i][None], ln1_b[i][None], wr, br, alpha=alpha)
        ya, yb = _moe(h1b, ri, cnt, moe_w_gate, moe_w_up, moe_w_down, i)
        h = _combine(h1, ya, yb, rw, ln2_g[i][None], ln2_b[i][None], alpha=alpha)

    y_prompt = h[:off_s].reshape(bp, lp_pad, D_MODEL)[:, N_META:lp]
    y_sample = h[off_s:off_s + bs * lsm].reshape(bs, lsm, D_MODEL)
    return (y_prompt, y_sample, jnp.stack(new_ssd_p), jnp.stack(new_conv_p), jnp.stack(new_gla_p),
            jnp.stack(new_ssd_s), jnp.stack(new_conv_s), jnp.stack(new_gla_s))
```

```python
import functools
import math

import jax
import jax.numpy as jnp
from jax import lax
from jax.experimental import pallas as pl
from jax.experimental.pallas import tpu as pltpu

F32 = jnp.float32
BF16 = jnp.bfloat16
I32 = jnp.int32
NT = (((1,), (1,)), ((), ()))
TN = (((0,), (0,)), ((), ()))
NN = (((1,), (0,)), ((), ()))

D_MODEL = 1024
N_META = 16
SSD_INNER = 2048
SSD_HEAD_DIM = 64
SSD_HEADS = 32
SSD_GROUPS = 4
SSD_STATE = 128
SSD_CONV_W = 4
SSD_CONV_DIM = 3072
GLA_HEADS = 4
GLA_KEY_DIM = 512
GLA_VAL_DIM = 1024
GLA_DK = 128
GLA_DV = 256
GLA_GATE_RANK = 16
GLA_GATE_NORM = 16.0
MOE_GROUPS = 4
MOE_EXP_PER_GROUP = 8
MOE_EXPERTS = 32
EXPERT_FF = 512
NORM_EPS = 1e-5

LANES = 128
VMEM_LIMIT_MB = 56
ROW_TILE = 256
POST_TILE = 1024
PROMPT_CHUNK = 128
PROMPT_STEP_CHUNKS = 2
GLA_SUB = 8
EXPERT_BLOCK = 256
EXPERT_STEP_BLOCKS = 2
INV_CHUNK = 1024


def _params(sem):
    return pltpu.CompilerParams(dimension_semantics=sem, vmem_limit_bytes=VMEM_LIMIT_MB << 20)


def _sigmoid(x):
    return 1.0 / (1.0 + jnp.exp(-x))


def _softplus(x):
    return jnp.maximum(x, 0.0) + jnp.log1p(jnp.exp(-jnp.abs(x)))


def _iota(shape, dim):
    return lax.broadcasted_iota(I32, shape, dim)


def _exact_dot(a_b, x, dims):
    hi = x.astype(BF16)
    r1 = x - hi.astype(F32)
    mid = r1.astype(BF16)
    lo = (r1 - mid.astype(F32)).astype(BF16)
    f = lambda t: lax.dot_general(a_b, t, dims, preferred_element_type=F32)
    return (f(hi) + f(mid)) + f(lo)


def _proj_kernel(x_ref, *refs):
    n = len(refs) // 2
    xb = x_ref[...].astype(BF16)
    for w_ref, o_ref in zip(refs[:n], refs[n:]):
        o_ref[...] = jnp.dot(xb, w_ref[...], preferred_element_type=F32).astype(o_ref.dtype)


def _proj(x, ws, out_dtypes):
    rows, k = x.shape
    grid = (rows // ROW_TILE,)
    in_specs = [pl.BlockSpec((ROW_TILE, k), lambda i: (i, 0))]
    in_specs += [pl.BlockSpec(w.shape, lambda i: (0, 0)) for w in ws]
    out_specs = [pl.BlockSpec((ROW_TILE, w.shape[1]), lambda i: (i, 0)) for w in ws]
    out_shape = [jax.ShapeDtypeStruct((rows, w.shape[1]), dt) for w, dt in zip(ws, out_dtypes)]
    return pl.pallas_call(
        _proj_kernel, grid=grid, in_specs=in_specs, out_specs=out_specs, out_shape=out_shape,
        compiler_params=_params(("parallel",)))(x, *ws)


def _ssd_kernel(z_ref, xbc_ref, dt_ref, cw_ref, cb_ref, dtb_ref, an_ref, dsk_ref, nw_ref, s0_ref, c0_ref,
                yprev_ref, y_ref, sout_ref, st_ref, xc_ref, u_ref, *, C, L):
    del yprev_ref
    c = pl.program_id(1)
    hist = 8

    @pl.when(c == 0)
    def _():
        st_ref[...] = s0_ref[0]
        xc_ref[0:hist, :] = c0_ref[0]

    @pl.when(c > 0)
    def _():
        xc_ref[0:hist, :] = xc_ref[C:C + hist, :]

    xc_ref[hist:hist + C, :] = xbc_ref[...].astype(F32)

    for j in range(0, SSD_CONV_DIM, 512):
        sl = slice(j, j + 512)
        acc = cb_ref[:, sl] + xc_ref[hist:hist + C, sl] * cw_ref[3:4, sl]
        for w in range(SSD_CONV_W - 1):
            off = hist - (SSD_CONV_W - 1) + w
            acc = acc + xc_ref[off:off + C, sl] * cw_ref[w:w + 1, sl]
        u_ref[:, sl] = acc * _sigmoid(acc)

    row = _iota((C, LANES), 0) + c * C
    dt = jnp.where(row < L, _softplus(dt_ref[...] + dtb_ref[...]), 0.0)
    loga = dt * an_ref[...]
    ti = _iota((C, C), 0)
    si = _iota((C, C), 1)
    tril = si <= ti
    eye_b = (_iota((LANES, LANES), 0) == _iota((LANES, LANES), 1)).astype(BF16)
    cum = _exact_dot(tril.astype(BF16), loga, NN)
    cum_t = _exact_dot(eye_b, cum, NT)
    dt_t = _exact_dot(eye_b, dt, NT)
    w_t = dt_t * jnp.exp(cum_t[:, C - 1:C] - cum_t)
    src_t = cum_t - jnp.log(dt_t)
    e_last = jnp.exp(cum[C - 1:C, :])
    lane = _iota((C, LANES), 1)
    lane_s = _iota((SSD_STATE, LANES), 1)
    lane_r = _iota((1, LANES), 1)
    hpg = SSD_HEADS // SSD_GROUPS

    for g in range(SSD_GROUPS):
        b_f = u_ref[:, SSD_INNER + g * SSD_STATE:SSD_INNER + (g + 1) * SSD_STATE]
        c_f = u_ref[:, SSD_INNER + (SSD_GROUPS + g) * SSD_STATE:SSD_INNER + (SSD_GROUPS + g + 1) * SSD_STATE]
        b_b = b_f.astype(BF16)
        cb = lax.dot_general(c_f.astype(BF16), b_b, NT, preferred_element_type=F32)
        b_t = lax.dot_general(eye_b, b_b, NT, preferred_element_type=F32)
        ypairs = []
        for jp in range(hpg // 2):
            p = g * (hpg // 2) + jp
            cs = slice(p * LANES, (p + 1) * LANES)
            xs_p = u_ref[:, cs]
            xs_b = xs_p.astype(BF16)
            st_p = st_ref[:, cs]
            st_b = st_p.astype(BF16)
            res, upd = [], []
            for h in (2 * p, 2 * p + 1):
                ccol = cum[:, h:h + 1]
                dec = jnp.where(tril, jnp.exp(ccol - src_t[h:h + 1, :]), 0.0)
                wmat = (cb * dec).astype(BF16)
                c_e = (c_f * jnp.exp(ccol)).astype(BF16)
                res.append(jnp.dot(wmat, xs_b, preferred_element_type=F32)
                           + jnp.dot(c_e, st_b, preferred_element_type=F32))
                b_w = (b_t * w_t[h:h + 1, :]).astype(BF16)
                upd.append(jnp.dot(b_w, xs_b, preferred_element_type=F32))
            y_p = jnp.where(lane < SSD_HEAD_DIM, res[0], res[1]) + dsk_ref[:, cs] * xs_p
            e_p = jnp.where(lane_r < SSD_HEAD_DIM, e_last[:, 2 * p:2 * p + 1], e_last[:, 2 * p + 1:2 * p + 2])
            st_ref[:, cs] = st_p * e_p + jnp.where(lane_s < SSD_HEAD_DIM, upd[0], upd[1])
            ypairs.append(y_p)
        gs = slice(g * 512, (g + 1) * 512)
        yg = jnp.concatenate(ypairs, axis=1)
        zg = z_ref[:, gs].astype(F32)
        yg = yg * (zg * _sigmoid(zg))
        ms = jnp.mean(yg * yg, axis=-1, keepdims=True)
        y_ref[:, gs] = (yg * lax.rsqrt(ms + NORM_EPS) * nw_ref[:, gs]).astype(y_ref.dtype)

    @pl.when(c == pl.num_programs(1) - 1)
    def _():
        sout_ref[0] = st_ref[...]


def _ssd_scan(z, xbc, dt, cw, cb, dtb, an, dsk, nw, s0_t, c0, y_prev, *, row_off, nseq, nwalk, L, Lpad, C):
    nc = Lpad // C
    base = row_off // C

    def rmap(b, c):
        return (base + b * nc + c, 0)

    const = lambda b, c: (0, 0)
    in_specs = [
        pl.BlockSpec((C, SSD_INNER), rmap),
        pl.BlockSpec((C, SSD_CONV_DIM), rmap),
        pl.BlockSpec((C, LANES), rmap),
        pl.BlockSpec((SSD_CONV_W, SSD_CONV_DIM), const),
        pl.BlockSpec((1, SSD_CONV_DIM), const),
        pl.BlockSpec((1, LANES), const),
        pl.BlockSpec((1, LANES), const),
        pl.BlockSpec((1, SSD_INNER), const),
        pl.BlockSpec((1, SSD_INNER), const),
        pl.BlockSpec((1, SSD_STATE, SSD_INNER), lambda b, c: (jnp.minimum(b, nseq - 1), 0, 0)),
        pl.BlockSpec((1, 8, SSD_CONV_DIM), lambda b, c: (jnp.minimum(b, nseq - 1), 0, 0)),
        pl.BlockSpec(memory_space=pl.ANY),
    ]
    out_specs = [
        pl.BlockSpec((C, SSD_INNER), rmap),
        pl.BlockSpec((1, SSD_STATE, SSD_INNER), lambda b, c: (b, 0, 0)),
    ]
    out_shape = [
        jax.ShapeDtypeStruct(y_prev.shape, y_prev.dtype),
        jax.ShapeDtypeStruct((nwalk, SSD_STATE, SSD_INNER), F32),
    ]
    scratch = [
        pltpu.VMEM((SSD_STATE, SSD_INNER), F32),
        pltpu.VMEM((C + 8, SSD_CONV_DIM), F32),
        pltpu.VMEM((C, SSD_CONV_DIM), F32),
    ]
    y, st = pl.pallas_call(
        functools.partial(_ssd_kernel, C=C, L=L), grid=(nwalk, nc), in_specs=in_specs, out_specs=out_specs,
        out_shape=out_shape, scratch_shapes=scratch, input_output_aliases={11: 0},
        compiler_params=_params(("arbitrary", "arbitrary")))(z, xbc, dt, cw, cb, dtb, an, dsk, nw, s0_t, c0, y_prev)
    return y, st[:nseq]


def _gla_kernel(q_ref, k_ref, v_ref, g_ref, lr_ref, w2_ref, b2_ref, nw_ref, s0_ref,
                oprev_ref, o_ref, sout_ref, st_ref, *, C, L, nsub):
    del oprev_ref
    c = pl.program_id(1)
    sub = GLA_SUB
    nb = C // sub

    @pl.when(c == 0)
    def _():
        st_ref[...] = s0_ref[0]

    ti = _iota((C, C), 0)
    si = _iota((C, C), 1)
    tril = si <= ti
    tril_b = tril.astype(BF16)
    diag_mask = jnp.logical_and(tril, (ti // sub) == (si // sub))
    lane_o = si % sub
    scale = GLA_DK ** -0.5

    intra = {}
    for s in range(nsub):
        rs = slice(s * C, (s + 1) * C)
        first_row = (c * nsub + s) * C
        gk = jnp.dot(lr_ref[rs, :].astype(BF16), w2_ref[...], preferred_element_type=F32) + b2_ref[...]
        logg = (jnp.minimum(gk, 0.0) - jnp.log1p(jnp.exp(-jnp.abs(gk)))) * (1.0 / GLA_GATE_NORM)
        logg = jnp.where(_iota((C, GLA_KEY_DIM), 0) + first_row < L, logg, 0.0)
        cum_all = _exact_dot(tril_b, logg, NN)
        valid_v = _iota((C, GLA_DV), 0) + first_row < L
        for h in range(GLA_HEADS):
            ks = slice(h * GLA_DK, (h + 1) * GLA_DK)
            vs = slice(h * GLA_DV, (h + 1) * GLA_DV)
            q = q_ref[rs, ks].astype(F32) * scale
            k = k_ref[rs, ks].astype(F32)
            v = jnp.where(valid_v, v_ref[rs, vs].astype(F32), 0.0)
            v_b = v.astype(BF16)
            cum = cum_all[:, ks]
            if nb > 1:
                q_parts, k_parts = [], []
                for i in range(1, nb):
                    r_i = cum[i * sub - 1:i * sub, :]
                    qi = q[i * sub:(i + 1) * sub, :] * jnp.exp(cum[i * sub:(i + 1) * sub, :] - r_i)
                    pieces = [jnp.zeros((i * sub, GLA_DK), F32), qi]
                    if C - (i + 1) * sub > 0:
                        pieces.append(jnp.zeros((C - (i + 1) * sub, GLA_DK), F32))
                    q_parts.append(jnp.concatenate(pieces, axis=0))
                    ki = k[0:i * sub, :] * jnp.exp(r_i - cum[0:i * sub, :])
                    k_parts.append(jnp.concatenate([ki, jnp.zeros((C - i * sub, GLA_DK), F32)], axis=0))
                q_st = jnp.concatenate(q_parts, axis=1).astype(BF16)
                k_st = jnp.concatenate(k_parts, axis=1).astype(BF16)
                a_off = lax.dot_general(q_st, k_st, NT, preferred_element_type=F32)
            else:
                a_off = jnp.zeros((C, C), F32)
            q3 = q.reshape(nb, sub, GLA_DK)
            k3 = k.reshape(nb, sub, GLA_DK)
            c3 = cum.reshape(nb, sub, GLA_DK)
            a_diag = jnp.zeros((C, C), F32)
            for o in range(sub):
                p_o = q3 * jnp.exp(jnp.minimum(c3 - c3[:, o:o + 1, :], 0.0)) * k3[:, o:o + 1, :]
                a_diag = jnp.where(lane_o == o, jnp.sum(p_o.reshape(C, GLA_DK), axis=-1, keepdims=True), a_diag)
            a = a_off + jnp.where(diag_mask, a_diag, 0.0)
            y_intra = jnp.dot(a.astype(BF16), v_b, preferred_element_type=F32)
            last = cum[C - 1:C, :]
            q_e = (q * jnp.exp(cum)).astype(BF16)
            k_e = (k * jnp.exp(last - cum)).astype(BF16)
            intra[s, h] = (y_intra, q_e, k_e, v_b, jnp.exp(last))

    for h in range(GLA_HEADS):
        vs = slice(h * GLA_DV, (h + 1) * GLA_DV)
        st = st_ref[vs, :]
        for s in range(nsub):
            rs = slice(s * C, (s + 1) * C)
            y_intra, q_e, k_e, v_b, e_last = intra[s, h]
            y = y_intra + lax.dot_general(q_e, st.astype(BF16), NT, preferred_element_type=F32)
            st = st * e_last + lax.dot_general(v_b, k_e, TN, preferred_element_type=F32)
            ms = jnp.mean(y * y, axis=-1, keepdims=True)
            gate = g_ref[rs, vs].astype(F32)
            o_ref[rs, vs] = (y * lax.rsqrt(ms + NORM_EPS) * nw_ref[...]
                             * (gate * _sigmoid(gate))).astype(o_ref.dtype)
        st_ref[vs, :] = st

    @pl.when(c == pl.num_programs(1) - 1)
    def _():
        sout_ref[0] = st_ref[...]


def _gla_scan(qkvg, lr, w2, b2, nw, s0_t, o_prev, *, row_off, nseq, nwalk, L, Lpad, C, nsub):
    blk = nsub * C
    nc = Lpad // blk
    base = row_off // blk
    const = lambda b, c: (0, 0)
    in_specs = [
        pl.BlockSpec((blk, GLA_KEY_DIM), lambda b, c: (base + b * nc + c, 0)),
        pl.BlockSpec((blk, GLA_KEY_DIM), lambda b, c: (base + b * nc + c, 1)),
        pl.BlockSpec((blk, GLA_VAL_DIM), lambda b, c: (base + b * nc + c, 1)),
        pl.BlockSpec((blk, GLA_VAL_DIM), lambda b, c: (base + b * nc + c, 2)),
        pl.BlockSpec((blk, LANES), lambda b, c: (base + b * nc + c, 0)),
        pl.BlockSpec((LANES, GLA_KEY_DIM), const),
        pl.BlockSpec((1, GLA_KEY_DIM), const),
        pl.BlockSpec((1, GLA_DV), const),
        pl.BlockSpec((1, GLA_VAL_DIM, GLA_DK), lambda b, c: (jnp.minimum(b, nseq - 1), 0, 0)),
        pl.BlockSpec(memory_space=pl.ANY),
    ]
    out_specs = [
        pl.BlockSpec((blk, GLA_VAL_DIM), lambda b, c: (base + b * nc + c, 0)),
        pl.BlockSpec((1, GLA_VAL_DIM, GLA_DK), lambda b, c: (b, 0, 0)),
    ]
    out_shape = [
        jax.ShapeDtypeStruct(o_prev.shape, o_prev.dtype),
        jax.ShapeDtypeStruct((nwalk, GLA_VAL_DIM, GLA_DK), F32),
    ]
    o, st = pl.pallas_call(
        functools.partial(_gla_kernel, C=C, L=L, nsub=nsub), grid=(nwalk, nc), in_specs=in_specs, out_specs=out_specs,
        out_shape=out_shape, scratch_shapes=[pltpu.VMEM((GLA_VAL_DIM, GLA_DK), F32)],
        input_output_aliases={9: 0},
        compiler_params=_params(("arbitrary", "arbitrary")))(qkvg, qkvg, qkvg, qkvg, lr, w2, b2, nw, s0_t, o_prev)
    return o, st[:nseq]


def _layer_norm(x, g, b):
    mu = jnp.mean(x, axis=-1, keepdims=True)
    xc = x - mu
    var = jnp.mean(xc * xc, axis=-1, keepdims=True)
    return xc * lax.rsqrt(var + NORM_EPS) * g + b


def _post_kernel(y_ref, wo_ref, h_ref, g_ref, b_ref, wr_ref, br_ref,
                 h1_ref, h1b_ref, ri_ref, rw_ref, cnt_ref, carry_ref, *, alpha):
    i = pl.program_id(0)
    tm = h_ref.shape[0]

    @pl.when(i == 0)
    def _():
        carry_ref[...] = jnp.zeros_like(carry_ref)

    mix = jnp.dot(y_ref[...], wo_ref[...], preferred_element_type=F32)
    h1 = _layer_norm(alpha * h_ref[...] + mix, g_ref[...], b_ref[...])
    h1_ref[...] = h1
    h_hi = h1.astype(BF16)
    h1b_ref[...] = h_hi
    logits = jnp.dot(h_hi, wr_ref[...], preferred_element_type=F32) + br_ref[...]
    lane = _iota((tm, LANES), 1)
    neg = -jnp.inf
    is_g = lane < MOE_GROUPS
    gl = jnp.where(is_g, logits, neg)
    gmax = jnp.max(gl, axis=-1, keepdims=True)
    gsel = jnp.min(jnp.where(gl == gmax, lane, LANES), axis=-1, keepdims=True)
    p_g = 1.0 / jnp.sum(jnp.where(is_g, jnp.exp(gl - gmax), 0.0), axis=-1, keepdims=True)
    eid = lane - MOE_GROUPS
    in_g = (eid >= 0) & (eid < MOE_EXPERTS) & ((eid // MOE_EXP_PER_GROUP) == gsel)
    el = jnp.where(in_g, logits, neg)
    v1 = jnp.max(el, axis=-1, keepdims=True)
    i1 = jnp.min(jnp.where(el == v1, lane, LANES), axis=-1, keepdims=True)
    el2 = jnp.where(lane == i1, neg, el)
    v2 = jnp.max(el2, axis=-1, keepdims=True)
    i2 = jnp.min(jnp.where(el2 == v2, lane, LANES), axis=-1, keepdims=True)
    t = jnp.exp(v2 - v1)
    w1 = p_g / (1.0 + t)
    w2 = p_g * t / (1.0 + t)
    e1 = i1 - MOE_GROUPS
    e2 = i2 - MOE_GROUPS
    oh = jnp.where(lane == e1, 1.0, 0.0) + jnp.where(lane == e2, 1.0, 0.0)
    strict = (_iota((tm, tm), 1) < _iota((tm, tm), 0)).astype(BF16)
    before = jnp.dot(strict, oh.astype(BF16), preferred_element_type=F32) + carry_ref[...]
    r1 = jnp.sum(jnp.where(lane == e1, before, 0.0), axis=-1, keepdims=True)
    r2 = jnp.sum(jnp.where(lane == e2, before, 0.0), axis=-1, keepdims=True)
    carry_ref[...] = carry_ref[...] + jnp.sum(oh, axis=0, keepdims=True)
    cnt_ref[...] = carry_ref[...]
    packed = jnp.where(lane == 0, e1.astype(F32), jnp.where(lane == 1, e2.astype(F32), jnp.where(
        lane == 2, r1, jnp.where(lane == 3, r2, 0.0))))
    ri_ref[...] = packed.T[0:8, :]
    rw_ref[...] = jnp.where(lane == 0, w1, jnp.where(lane == 1, w2, 0.0))


def _post_mixer(y, wo, h, g, b, wr, br, *, alpha):
    rows, kin = y.shape
    grid = (rows // POST_TILE,)
    rmap = lambda i: (i, 0)
    const = lambda i: (0, 0)
    in_specs = [
        pl.BlockSpec((POST_TILE, kin), rmap),
        pl.BlockSpec((kin, D_MODEL), const),
        pl.BlockSpec((POST_TILE, D_MODEL), rmap),
        pl.BlockSpec((1, D_MODEL), const),
        pl.BlockSpec((1, D_MODEL), const),
        pl.BlockSpec((D_MODEL, LANES), const),
        pl.BlockSpec((1, LANES), const),
    ]
    out_specs = [
        pl.BlockSpec((POST_TILE, D_MODEL), rmap),
        pl.BlockSpec((POST_TILE, D_MODEL), rmap),
        pl.BlockSpec((8, POST_TILE), lambda i: (0, i)),
        pl.BlockSpec((POST_TILE, LANES), rmap),
        pl.BlockSpec((1, LANES), const),
    ]
    out_shape = [
        jax.ShapeDtypeStruct((rows, D_MODEL), F32),
        jax.ShapeDtypeStruct((rows, D_MODEL), BF16),
        jax.ShapeDtypeStruct((8, rows), F32),
        jax.ShapeDtypeStruct((rows, LANES), F32),
        jax.ShapeDtypeStruct((1, LANES), F32),
    ]
    return pl.pallas_call(
        functools.partial(_post_kernel, alpha=alpha), grid=grid, in_specs=in_specs, out_specs=out_specs,
        out_shape=out_shape, scratch_shapes=[pltpu.VMEM((1, LANES), F32)],
        compiler_params=_params(("arbitrary",)))(y, wo, h, g, b, wr, br)


def _expert_kernel(be_ref, nu_ref, x_ref, *refs):
    nb = EXPERT_STEP_BLOCKS
    w_refs, o_ref, (wg_b, wu_b, wd_b) = refs[:3 * nb], refs[3 * nb], refs[3 * nb + 1:]
    i = pl.program_id(0)
    for half in range(nb):
        blk = i * nb + half
        wg_ref, wu_ref, wd_ref = w_refs[3 * half:3 * half + 3]
        rs = slice(half * EXPERT_BLOCK, (half + 1) * EXPERT_BLOCK)
        prev = be_ref[jnp.maximum(blk - 1, 0)]

        @pl.when(jnp.logical_or(blk == 0, be_ref[blk] != prev))
        def _():
            wg_b[...] = wg_ref[...].astype(BF16)
            wu_b[...] = wu_ref[...].astype(BF16)
            wd_b[...] = wd_ref[...].astype(BF16)

        @pl.when(blk < nu_ref[0])
        def _():
            x = x_ref[rs, :]
            hg = jnp.dot(x, wg_b[...], preferred_element_type=F32)
            hu = jnp.dot(x, wu_b[...], preferred_element_type=F32)
            hh = (hg * _sigmoid(hg) * hu).astype(BF16)
            o_ref[rs, :] = jnp.dot(hh, wd_b[...], preferred_element_type=F32).astype(o_ref.dtype)

        @pl.when(blk >= nu_ref[0])
        def _():
            o_ref[rs, :] = jnp.zeros((EXPERT_BLOCK, D_MODEL), o_ref.dtype)


def _experts(block_e, n_used, xb, w_gate, w_up, w_down, layer):
    nb = EXPERT_STEP_BLOCKS
    step_rows = nb * EXPERT_BLOCK
    nstep = xb.shape[0] // step_rows
    w_specs, w_args = [], []
    for half in range(nb):
        wmap = lambda i, be, nu, half=half: (layer, be[i * nb + half], 0, 0)
        w_specs += [pl.BlockSpec((None, None, D_MODEL, EXPERT_FF), wmap),
                    pl.BlockSpec((None, None, D_MODEL, EXPERT_FF), wmap),
                    pl.BlockSpec((None, None, EXPERT_FF, D_MODEL), wmap)]
        w_args += [w_gate, w_up, w_down]
    grid_spec = pltpu.PrefetchScalarGridSpec(
        num_scalar_prefetch=2, grid=(nstep,),
        in_specs=[pl.BlockSpec((step_rows, D_MODEL), lambda i, be, nu: (i, 0))] + w_specs,
        out_specs=pl.BlockSpec((step_rows, D_MODEL), lambda i, be, nu: (i, 0)),
        scratch_shapes=[pltpu.VMEM((D_MODEL, EXPERT_FF), BF16), pltpu.VMEM((D_MODEL, EXPERT_FF), BF16),
                        pltpu.VMEM((EXPERT_FF, D_MODEL), BF16)])
    return pl.pallas_call(
        _expert_kernel, grid_spec=grid_spec,
        out_shape=jax.ShapeDtypeStruct(xb.shape, BF16),
        compiler_params=_params(("arbitrary",)))(block_e, n_used, xb, *w_args)


def _combine_kernel(h_ref, ya_ref, yb_ref, rw_ref, g_ref, b_ref, o_ref, *, alpha):
    rw = rw_ref[...]
    ffn = ya_ref[...].astype(F32) * rw[:, 0:1] + yb_ref[...].astype(F32) * rw[:, 1:2]
    o_ref[...] = _layer_norm(alpha * h_ref[...] + ffn, g_ref[...], b_ref[...])


def _combine(h1, ya, yb, rw, g, b, *, alpha):
    rows = h1.shape[0]
    rmap = lambda i: (i, 0)
    const = lambda i: (0, 0)
    in_specs = [
        pl.BlockSpec((POST_TILE, D_MODEL), rmap), pl.BlockSpec((POST_TILE, D_MODEL), rmap),
        pl.BlockSpec((POST_TILE, D_MODEL), rmap), pl.BlockSpec((POST_TILE, LANES), rmap),
        pl.BlockSpec((1, D_MODEL), const), pl.BlockSpec((1, D_MODEL), const),
    ]
    return pl.pallas_call(
        functools.partial(_combine_kernel, alpha=alpha), grid=(rows // POST_TILE,), in_specs=in_specs,
        out_specs=pl.BlockSpec((POST_TILE, D_MODEL), rmap),
        out_shape=jax.ShapeDtypeStruct((rows, D_MODEL), F32),
        compiler_params=_params(("parallel",)))(h1, ya, yb, rw, g, b)


def _round_up(x, m):
    return (x + m - 1) // m * m


def _pad_cols(w, n):
    return jnp.pad(w, ((0, 0), (0, n - w.shape[1])))


def _take_rows(x, idx):
    return x.at[idx].get(mode='promise_in_bounds')


def _invert_kernel(npair_ref, dest_ref, init_ref, out_ref, inv_smem, buf0, buf1, sems, *, n_tok):
    nch = dest_ref.shape[0]
    bufs = (buf0, buf1)

    def chunk_copy(ch, slot):
        return pltpu.make_async_copy(dest_ref.at[ch], bufs[slot], sems.at[slot])

    init_copy = pltpu.make_async_copy(init_ref, inv_smem, sems.at[2])
    init_copy.start()
    chunk_copy(0, 0).start()
    init_copy.wait()

    def per_pair(pair, carry):
        for slot in range(2):
            ch = 2 * pair + slot
            chunk_copy(ch, slot).wait()

            @pl.when(ch + 1 < nch)
            def _():
                chunk_copy(ch + 1, 1 - slot).start()

            flat0 = ch * INV_CHUNK
            tok0 = jnp.where(flat0 >= n_tok, flat0 - n_tok, flat0)
            buf = bufs[slot]

            def per_slot(j, inner):
                inv_smem[buf[j]] = tok0 + j
                return inner

            lax.fori_loop(0, INV_CHUNK, per_slot, 0, unroll=16)
        return carry

    lax.fori_loop(0, npair_ref[0], per_pair, 0)
    out_copy = pltpu.make_async_copy(inv_smem, out_ref, sems.at[2])
    out_copy.start()
    out_copy.wait()


def _invert_slots(dest, init):
    n_tok = dest.shape[1]
    assert n_tok % INV_CHUNK == 0 and init.shape[0] % INV_CHUNK == 0
    any_spec = pl.BlockSpec(memory_space=pl.ANY)
    npair = jnp.full((1,), 2 * n_tok // INV_CHUNK // 2, I32)
    grid_spec = pltpu.PrefetchScalarGridSpec(
        num_scalar_prefetch=1, grid=(), in_specs=[any_spec, any_spec], out_specs=any_spec,
        scratch_shapes=[pltpu.SMEM(init.shape, I32), pltpu.SMEM((INV_CHUNK,), I32), pltpu.SMEM((INV_CHUNK,), I32),
                        pltpu.SemaphoreType.DMA((3,))])
    return pl.pallas_call(
        functools.partial(_invert_kernel, n_tok=n_tok), grid_spec=grid_spec,
        out_shape=jax.ShapeDtypeStruct(init.shape, I32),
    )(npair, dest.reshape(-1, INV_CHUNK), init)


def _moe(h1b, ri, cnt, w_gate, w_up, w_down, layer):
    rows = h1b.shape[0]
    nblk = _round_up((2 * rows + MOE_EXPERTS * (EXPERT_BLOCK - 1) + EXPERT_BLOCK - 1) // EXPERT_BLOCK,
                     EXPERT_STEP_BLOCKS)
    counts = cnt[0, :MOE_EXPERTS].astype(I32)
    pcounts = (counts + EXPERT_BLOCK - 1) // EXPERT_BLOCK * EXPERT_BLOCK
    pends = jnp.cumsum(pcounts)
    pstarts = pends - pcounts
    ri = ri.astype(I32)
    dest = _take_rows(pstarts, ri[0:2].reshape(-1)).reshape(2, rows) + ri[2:4]
    nslot = _round_up(nblk * EXPERT_BLOCK, INV_CHUNK)
    row_tok = _invert_slots(dest, jnp.arange(nslot, dtype=I32) % rows)[:nblk * EXPERT_BLOCK]
    blk_start = jnp.arange(nblk, dtype=I32) * EXPERT_BLOCK
    block_e = jnp.minimum(jnp.sum((pends[None, :] <= blk_start[:, None]).astype(I32), axis=1), MOE_EXPERTS - 1)
    n_used = (pends[-1:] // EXPERT_BLOCK).astype(I32)
    xb = _take_rows(h1b, row_tok)
    yb = _experts(block_e, n_used, xb, w_gate, w_up, w_down, layer)
    return _take_rows(yb, dest[0]), _take_rows(yb, dest[1])


def kernel(x_prompt, x_sample, state_ssd, state_ssd_conv, state_gla, meta_tokens, ssd_w_in, ssd_conv_w, ssd_conv_b, ssd_dt_bias, ssd_A_log, ssd_D, ssd_norm_w, ssd_w_out, gla_w_in, gla_w_gk2, gla_b_gk2, gla_norm_w, gla_w_out, ln1_g, ln1_b, moe_w_grp, moe_b_grp, moe_w_exp, moe_b_exp, moe_w_gate, moe_w_up, moe_w_down, ln2_g, ln2_b):
    bp, seq, _ = x_prompt.shape
    bs, lsm, _ = x_sample.shape
    depth = ln1_g.shape[0]
    alpha = (2.0 * depth) ** 0.25
    lp = N_META + seq
    cp = PROMPT_CHUNK
    lp_pad = _round_up(lp, PROMPT_STEP_CHUNKS * cp)
    cs = lsm
    off_s = bp * lp_pad
    assert off_s % cs == 0 and cs % GLA_SUB == 0 and cp % GLA_SUB == 0
    rows = _round_up(off_s + bs * lsm, max(POST_TILE, INV_CHUNK))

    pieces = []
    for b in range(bp):
        pieces += [meta_tokens.astype(F32), x_prompt[b], jnp.zeros((lp_pad - lp, D_MODEL), F32)]
    pieces += [x_sample.reshape(bs * lsm, D_MODEL), jnp.zeros((rows - off_s - bs * lsm, D_MODEL), F32)]
    h = jnp.concatenate(pieces, axis=0)

    groups = (dict(row_off=0, nseq=bp, nwalk=bp, L=lp, Lpad=lp_pad, C=cp),
              dict(row_off=off_s, nseq=bs, nwalk=(rows - off_s) // cs, L=lsm, Lpad=lsm, C=cs))
    ymix_ssd = jnp.zeros((rows, SSD_INNER), BF16)
    ymix_gla = jnp.zeros((rows, GLA_VAL_DIM), BF16)

    new_ssd_p, new_ssd_s, new_conv_p, new_conv_s, new_gla_p, new_gla_s = [], [], [], [], [], []
    for i in range(depth):
        j = i // 2
        if i % 2 == 0:
            w_in = ssd_w_in[j].astype(BF16)
            z, xbc, dtr = _proj(
                h, [w_in[:, :SSD_INNER], w_in[:, SSD_INNER:SSD_INNER + SSD_CONV_DIM],
                    _pad_cols(w_in[:, SSD_INNER + SSD_CONV_DIM:], LANES)], [F32, F32, F32])
            cw = ssd_conv_w[j]
            cb = ssd_conv_b[j][None]
            dtb = _pad_cols(ssd_dt_bias[j][None], LANES)
            an = _pad_cols(-jnp.exp(ssd_A_log[j].astype(F32))[None], LANES)
            dsk = jnp.repeat(ssd_D[j], SSD_HEAD_DIM)[None]
            nw = ssd_norm_w[j][None]
            s0s = [jnp.zeros((bp, SSD_STATE, SSD_INNER), F32),
                   jnp.swapaxes(state_ssd[j].reshape(bs, SSD_INNER, SSD_STATE), 1, 2)]
            c0s = [jnp.zeros((bp, 8, SSD_CONV_DIM), F32),
                   jnp.pad(state_ssd_conv[j], ((0, 0), (8 - (SSD_CONV_W - 1), 0), (0, 0)))]
            for grp, s0, c0, acc_s, acc_c in zip(groups, s0s, c0s, (new_ssd_p, new_ssd_s), (new_conv_p, new_conv_s)):
                ymix_ssd, st = _ssd_scan(z, xbc, dtr, cw, cb, dtb, an, dsk, nw, s0, c0, ymix_ssd, **grp)
                n, L, Lpad = grp['nseq'], grp['L'], grp['Lpad']
                acc_s.append(jnp.swapaxes(st, 1, 2).reshape(n, SSD_HEADS, SSD_HEAD_DIM, SSD_STATE))
                nconv = SSD_CONV_W - 1
                last = (grp['row_off'] + jnp.arange(n, dtype=I32)[:, None] * Lpad + (L - nconv)
                        + jnp.arange(nconv, dtype=I32)[None, :])
                acc_c.append(_take_rows(xbc, last.reshape(-1)).reshape(n, nconv, SSD_CONV_DIM))
            ymix = ymix_ssd
            wo = ssd_w_out[j].astype(BF16)
        else:
            w_in = gla_w_in[j].astype(BF16)
            nq = 2 * GLA_KEY_DIM + 2 * GLA_VAL_DIM
            qkvg, lr = _proj(h, [w_in[:, :nq], _pad_cols(w_in[:, nq:], LANES)], [F32, F32])
            w2 = jnp.pad(gla_w_gk2[j], ((0, LANES - GLA_GATE_RANK), (0, 0))).astype(BF16)
            b2 = gla_b_gk2[j][None]
            nw = gla_norm_w[j][None]
            s0s = [jnp.zeros((bp, GLA_VAL_DIM, GLA_DK), F32),
                   jnp.swapaxes(state_gla[j], 2, 3).reshape(bs, GLA_VAL_DIM, GLA_DK)]
            for grp, s0, acc, nsub in zip(groups, s0s, (new_gla_p, new_gla_s), (PROMPT_STEP_CHUNKS, 1)):
                ymix_gla, st = _gla_scan(qkvg, lr, w2, b2, nw, s0, ymix_gla, nsub=nsub, **grp)
                acc.append(jnp.swapaxes(st.reshape(grp['nseq'], GLA_HEADS, GLA_DV, GLA_DK), 2, 3))
            ymix = ymix_gla
            wo = gla_w_out[j].astype(BF16)

        wr = _pad_cols(jnp.concatenate([moe_w_grp[i], moe_w_exp[i]], axis=1), LANES).astype(BF16)
        br = _pad_cols(jnp.concatenate([moe_b_grp[i], moe_b_exp[i]])[None], LANES)
        h1, h1b, ri, rw, cnt = _post_mixer(ymix, wo, h, ln1_g[i][None], ln1_b[i][None], wr, br, alpha=alpha)
        ya, yb = _moe(h1b, ri, cnt, moe_w_gate, moe_w_up, moe_w_down, i)
        h = _combine(h1, ya, yb, rw, ln2_g[i][None], ln2_b[i][None], alpha=alpha)

    y_prompt = h[:off_s].reshape(bp, lp_pad, D_MODEL)[:, N_META:lp]
    y_sample = h[off_s:off_s + bs * lsm].reshape(bs, lsm, D_MODEL)
    return (y_prompt, y_sample, jnp.stack(new_ssd_p), jnp.stack(new_conv_p), jnp.stack(new_gla_p),
            jnp.stack(new_ssd_s), jnp.stack(new_conv_s), jnp.stack(new_gla_s))
```

```python
import functools

import jax
import jax.numpy as jnp
from jax import lax
from jax.experimental import pallas as pl
from jax.experimental.pallas import tpu as pltpu

F32 = jnp.float32
BF16 = jnp.bfloat16
I32 = jnp.int32
NT = (((1,), (1,)), ((), ()))
TN = (((0,), (0,)), ((), ()))
NN = (((1,), (0,)), ((), ()))

D_MODEL = 1024
N_META = 16
SSD_INNER = 2048
SSD_HEAD_DIM = 64
SSD_HEADS = 32
SSD_GROUPS = 4
SSD_STATE = 128
SSD_CONV_W = 4
SSD_CONV_DIM = 3072
GLA_HEADS = 4
GLA_KEY_DIM = 512
GLA_VAL_DIM = 1024
GLA_DK = 128
GLA_DV = 256
GLA_GATE_RANK = 16
GLA_GATE_NORM = 16.0
MOE_GROUPS = 4
MOE_EXP_PER_GROUP = 8
MOE_EXPERTS = 32
EXPERT_FF = 512
NORM_EPS = 1e-5

LANES = 128
VMEM_LIMIT_MB = 56
ROW_TILE = 512
POST_TILE = 1024
PROMPT_CHUNK = 128
PROMPT_STEP_CHUNKS = 2
GLA_SUB = 8
EXPERT_BLOCK = 256
EXPERT_STEP_BLOCKS = 2
INV_CHUNK = 1024


def _params(sem):
    return pltpu.CompilerParams(dimension_semantics=sem, vmem_limit_bytes=VMEM_LIMIT_MB << 20)


def _sigmoid(x):
    return 1.0 / (1.0 + jnp.exp(-x))


def _softplus(x):
    return jnp.maximum(x, 0.0) + jnp.log1p(jnp.exp(-jnp.abs(x)))


def _iota(shape, dim):
    return lax.broadcasted_iota(I32, shape, dim)


def _exact_dot(a_b, x, dims):
    hi = x.astype(BF16)
    r1 = x - hi.astype(F32)
    mid = r1.astype(BF16)
    lo = (r1 - mid.astype(F32)).astype(BF16)
    f = lambda t: lax.dot_general(a_b, t, dims, preferred_element_type=F32)
    return (f(hi) + f(mid)) + f(lo)


def _proj_kernel(x_ref, *refs):
    n = len(refs) // 2
    xb = x_ref[...].astype(BF16)
    for w_ref, o_ref in zip(refs[:n], refs[n:]):
        o_ref[...] = jnp.dot(xb, w_ref[...], preferred_element_type=F32).astype(o_ref.dtype)


def _proj(x, ws, out_dtypes):
    rows, k = x.shape
    grid = (rows // ROW_TILE,)
    in_specs = [pl.BlockSpec((ROW_TILE, k), lambda i: (i, 0))]
    in_specs += [pl.BlockSpec(w.shape, lambda i: (0, 0)) for w in ws]
    out_specs = [pl.BlockSpec((ROW_TILE, w.shape[1]), lambda i: (i, 0)) for w in ws]
    out_shape = [jax.ShapeDtypeStruct((rows, w.shape[1]), dt) for w, dt in zip(ws, out_dtypes)]
    return pl.pallas_call(
        _proj_kernel, grid=grid, in_specs=in_specs, out_specs=out_specs, out_shape=out_shape,
        compiler_params=_params(("parallel",)))(x, *ws)


def _ssd_kernel(z_ref, xbc_ref, dt_ref, cw_ref, cb_ref, dtb_ref, an_ref, dsk_ref, nw_ref, s0_ref, c0_ref,
                yprev_ref, y_ref, sout_ref, st_ref, xc_ref, u_ref, *, C, L):
    del yprev_ref
    c = pl.program_id(1)
    hist = 8

    @pl.when(c == 0)
    def _():
        st_ref[...] = s0_ref[0]
        xc_ref[0:hist, :] = c0_ref[0]

    @pl.when(c > 0)
    def _():
        xc_ref[0:hist, :] = xc_ref[C:C + hist, :]

    xc_ref[hist:hist + C, :] = xbc_ref[...].astype(F32)

    for j in range(0, SSD_CONV_DIM, 512):
        sl = slice(j, j + 512)
        acc = cb_ref[:, sl] + xc_ref[hist:hist + C, sl] * cw_ref[3:4, sl]
        for w in range(SSD_CONV_W - 1):
            off = hist - (SSD_CONV_W - 1) + w
            acc = acc + xc_ref[off:off + C, sl] * cw_ref[w:w + 1, sl]
        u_ref[:, sl] = acc * _sigmoid(acc)

    row = _iota((C, LANES), 0) + c * C
    dt = jnp.where(row < L, _softplus(dt_ref[...] + dtb_ref[...]), 0.0)
    loga = dt * an_ref[...]
    ti = _iota((C, C), 0)
    si = _iota((C, C), 1)
    tril = si <= ti
    eye_b = (_iota((LANES, LANES), 0) == _iota((LANES, LANES), 1)).astype(BF16)
    cum = _exact_dot(tril.astype(BF16), loga, NN)
    cum_t = _exact_dot(eye_b, cum, NT)
    dt_t = _exact_dot(eye_b, dt, NT)
    w_t = dt_t * jnp.exp(cum_t[:, C - 1:C] - cum_t)
    src_t = cum_t - jnp.log(dt_t)
    e_last = jnp.exp(cum[C - 1:C, :])
    lane = _iota((C, LANES), 1)
    lane_s = _iota((SSD_STATE, LANES), 1)
    lane_r = _iota((1, LANES), 1)
    hpg = SSD_HEADS // SSD_GROUPS

    for g in range(SSD_GROUPS):
        b_f = u_ref[:, SSD_INNER + g * SSD_STATE:SSD_INNER + (g + 1) * SSD_STATE]
        c_f = u_ref[:, SSD_INNER + (SSD_GROUPS + g) * SSD_STATE:SSD_INNER + (SSD_GROUPS + g + 1) * SSD_STATE]
        b_b = b_f.astype(BF16)
        cb = lax.dot_general(c_f.astype(BF16), b_b, NT, preferred_element_type=F32)
        b_t = lax.dot_general(eye_b, b_b, NT, preferred_element_type=F32)
        ypairs = []
        for jp in range(hpg // 2):
            p = g * (hpg // 2) + jp
            cs = slice(p * LANES, (p + 1) * LANES)
            xs_p = u_ref[:, cs]
            xs_b = xs_p.astype(BF16)
            st_p = st_ref[:, cs]
            st_b = st_p.astype(BF16)
            res, upd = [], []
            for h in (2 * p, 2 * p + 1):
                ccol = cum[:, h:h + 1]
                dec = jnp.where(tril, jnp.exp(ccol - src_t[h:h + 1, :]), 0.0)
                wmat = (cb * dec).astype(BF16)
                c_e = (c_f * jnp.exp(ccol)).astype(BF16)
                res.append(jnp.dot(wmat, xs_b, preferred_element_type=F32)
                           + jnp.dot(c_e, st_b, preferred_element_type=F32))
                b_w = (b_t * w_t[h:h + 1, :]).astype(BF16)
                upd.append(jnp.dot(b_w, xs_b, preferred_element_type=F32))
            y_p = jnp.where(lane < SSD_HEAD_DIM, res[0], res[1]) + dsk_ref[:, cs] * xs_p
            e_p = jnp.where(lane_r < SSD_HEAD_DIM, e_last[:, 2 * p:2 * p + 1], e_last[:, 2 * p + 1:2 * p + 2])
            st_ref[:, cs] = st_p * e_p + jnp.where(lane_s < SSD_HEAD_DIM, upd[0], upd[1])
            ypairs.append(y_p)
        gs = slice(g * 512, (g + 1) * 512)
        yg = jnp.concatenate(ypairs, axis=1)
        zg = z_ref[:, gs].astype(F32)
        yg = yg * (zg * _sigmoid(zg))
        ms = jnp.mean(yg * yg, axis=-1, keepdims=True)
        y_ref[:, gs] = (yg * lax.rsqrt(ms + NORM_EPS) * nw_ref[:, gs]).astype(y_ref.dtype)

    @pl.when(c == pl.num_programs(1) - 1)
    def _():
        sout_ref[0] = st_ref[...]


def _ssd_scan(z, xbc, dt, cw, cb, dtb, an, dsk, nw, s0_t, c0, y_prev, *, row_off, nseq, nwalk, L, Lpad, C):
    nc = Lpad // C
    base = row_off // C

    def rmap(b, c):
        return (base + b * nc + c, 0)

    const = lambda b, c: (0, 0)
    in_specs = [
        pl.BlockSpec((C, SSD_INNER), rmap),
        pl.BlockSpec((C, SSD_CONV_DIM), rmap),
        pl.BlockSpec((C, LANES), rmap),
        pl.BlockSpec((SSD_CONV_W, SSD_CONV_DIM), const),
        pl.BlockSpec((1, SSD_CONV_DIM), const),
        pl.BlockSpec((1, LANES), const),
        pl.BlockSpec((1, LANES), const),
        pl.BlockSpec((1, SSD_INNER), const),
        pl.BlockSpec((1, SSD_INNER), const),
        pl.BlockSpec((1, SSD_STATE, SSD_INNER), lambda b, c: (jnp.minimum(b, nseq - 1), 0, 0)),
        pl.BlockSpec((1, 8, SSD_CONV_DIM), lambda b, c: (jnp.minimum(b, nseq - 1), 0, 0)),
        pl.BlockSpec(memory_space=pl.ANY),
    ]
    out_specs = [
        pl.BlockSpec((C, SSD_INNER), rmap),
        pl.BlockSpec((1, SSD_STATE, SSD_INNER), lambda b, c: (b, 0, 0)),
    ]
    out_shape = [
        jax.ShapeDtypeStruct(y_prev.shape, y_prev.dtype),
        jax.ShapeDtypeStruct((nwalk, SSD_STATE, SSD_INNER), F32),
    ]
    scratch = [
        pltpu.VMEM((SSD_STATE, SSD_INNER), F32),
        pltpu.VMEM((C + 8, SSD_CONV_DIM), F32),
        pltpu.VMEM((C, SSD_CONV_DIM), F32),
    ]
    y, st = pl.pallas_call(
        functools.partial(_ssd_kernel, C=C, L=L), grid=(nwalk, nc), in_specs=in_specs, out_specs=out_specs,
        out_shape=out_shape, scratch_shapes=scratch, input_output_aliases={11: 0},
        compiler_params=_params(("arbitrary", "arbitrary")))(z, xbc, dt, cw, cb, dtb, an, dsk, nw, s0_t, c0, y_prev)
    return y, st[:nseq]


def _gla_kernel(q_ref, k_ref, v_ref, g_ref, lr_ref, w2_ref, b2_ref, nw_ref, s0_ref,
                oprev_ref, o_ref, sout_ref, st_ref, *, C, L, nsub):
    del oprev_ref
    c = pl.program_id(1)
    sub = GLA_SUB
    nb = C // sub

    @pl.when(c == 0)
    def _():
        st_ref[...] = s0_ref[0]

    ti = _iota((C, C), 0)
    si = _iota((C, C), 1)
    tril = si <= ti
    tril_b = tril.astype(BF16)
    diag_mask = jnp.logical_and(tril, (ti // sub) == (si // sub))
    lane_o = si % sub
    scale = GLA_DK ** -0.5

    intra = {}
    for s in range(nsub):
        rs = slice(s * C, (s + 1) * C)
        first_row = (c * nsub + s) * C
        gk = jnp.dot(lr_ref[rs, :].astype(BF16), w2_ref[...], preferred_element_type=F32) + b2_ref[...]
        logg = (jnp.minimum(gk, 0.0) - jnp.log1p(jnp.exp(-jnp.abs(gk)))) * (1.0 / GLA_GATE_NORM)
        logg = jnp.where(_iota((C, GLA_KEY_DIM), 0) + first_row < L, logg, 0.0)
        cum_all = _exact_dot(tril_b, logg, NN)
        valid_v = _iota((C, GLA_DV), 0) + first_row < L
        for h in range(GLA_HEADS):
            ks = slice(h * GLA_DK, (h + 1) * GLA_DK)
            vs = slice(h * GLA_DV, (h + 1) * GLA_DV)
            q = q_ref[rs, ks].astype(F32) * scale
            k = k_ref[rs, ks].astype(F32)
            v = jnp.where(valid_v, v_ref[rs, vs].astype(F32), 0.0)
            v_b = v.astype(BF16)
            cum = cum_all[:, ks]
            if nb > 1:
                q_parts, k_parts = [], []
                for i in range(1, nb):
                    r_i = cum[i * sub - 1:i * sub, :]
                    qi = q[i * sub:(i + 1) * sub, :] * jnp.exp(cum[i * sub:(i + 1) * sub, :] - r_i)
                    pieces = [jnp.zeros((i * sub, GLA_DK), F32), qi]
                    if C - (i + 1) * sub > 0:
                        pieces.append(jnp.zeros((C - (i + 1) * sub, GLA_DK), F32))
                    q_parts.append(jnp.concatenate(pieces, axis=0))
                    ki = k[0:i * sub, :] * jnp.exp(r_i - cum[0:i * sub, :])
                    k_parts.append(jnp.concatenate([ki, jnp.zeros((C - i * sub, GLA_DK), F32)], axis=0))
                q_st = jnp.concatenate(q_parts, axis=1).astype(BF16)
                k_st = jnp.concatenate(k_parts, axis=1).astype(BF16)
                a_off = lax.dot_general(q_st, k_st, NT, preferred_element_type=F32)
            else:
                a_off = jnp.zeros((C, C), F32)
            q3 = q.reshape(nb, sub, GLA_DK)
            k3 = k.reshape(nb, sub, GLA_DK)
            c3 = cum.reshape(nb, sub, GLA_DK)
            a_diag = jnp.zeros((C, C), F32)
            for o in range(sub):
                p_o = q3 * jnp.exp(jnp.minimum(c3 - c3[:, o:o + 1, :], 0.0)) * k3[:, o:o + 1, :]
                a_diag = jnp.where(lane_o == o, jnp.sum(p_o.reshape(C, GLA_DK), axis=-1, keepdims=True), a_diag)
            a = a_off + jnp.where(diag_mask, a_diag, 0.0)
            y_intra = jnp.dot(a.astype(BF16), v_b, preferred_element_type=F32)
            last = cum[C - 1:C, :]
            q_e = (q * jnp.exp(cum)).astype(BF16)
            k_e = (k * jnp.exp(last - cum)).astype(BF16)
            intra[s, h] = (y_intra, q_e, k_e, v_b, jnp.exp(last))

    for h in range(GLA_HEADS):
        vs = slice(h * GLA_DV, (h + 1) * GLA_DV)
        st = st_ref[vs, :]
        for s in range(nsub):
            rs = slice(s * C, (s + 1) * C)
            y_intra, q_e, k_e, v_b, e_last = intra[s, h]
            y = y_intra + lax.dot_general(q_e, st.astype(BF16), NT, preferred_element_type=F32)
            st = st * e_last + lax.dot_general(v_b, k_e, TN, preferred_element_type=F32)
            ms = jnp.mean(y * y, axis=-1, keepdims=True)
            gate = g_ref[rs, vs].astype(F32)
            o_ref[rs, vs] = (y * lax.rsqrt(ms + NORM_EPS) * nw_ref[...]
                             * (gate * _sigmoid(gate))).astype(o_ref.dtype)
        st_ref[vs, :] = st

    @pl.when(c == pl.num_programs(1) - 1)
    def _():
        sout_ref[0] = st_ref[...]


def _gla_scan(qkvg, lr, w2, b2, nw, s0_t, o_prev, *, row_off, nseq, nwalk, L, Lpad, C, nsub):
    blk = nsub * C
    nc = Lpad // blk
    base = row_off // blk
    const = lambda b, c: (0, 0)
    in_specs = [
        pl.BlockSpec((blk, GLA_KEY_DIM), lambda b, c: (base + b * nc + c, 0)),
        pl.BlockSpec((blk, GLA_KEY_DIM), lambda b, c: (base + b * nc + c, 1)),
        pl.BlockSpec((blk, GLA_VAL_DIM), lambda b, c: (base + b * nc + c, 1)),
        pl.BlockSpec((blk, GLA_VAL_DIM), lambda b, c: (base + b * nc + c, 2)),
        pl.BlockSpec((blk, LANES), lambda b, c: (base + b * nc + c, 0)),
        pl.BlockSpec((LANES, GLA_KEY_DIM), const),
        pl.BlockSpec((1, GLA_KEY_DIM), const),
        pl.BlockSpec((1, GLA_DV), const),
        pl.BlockSpec((1, GLA_VAL_DIM, GLA_DK), lambda b, c: (jnp.minimum(b, nseq - 1), 0, 0)),
        pl.BlockSpec(memory_space=pl.ANY),
    ]
    out_specs = [
        pl.BlockSpec((blk, GLA_VAL_DIM), lambda b, c: (base + b * nc + c, 0)),
        pl.BlockSpec((1, GLA_VAL_DIM, GLA_DK), lambda b, c: (b, 0, 0)),
    ]
    out_shape = [
        jax.ShapeDtypeStruct(o_prev.shape, o_prev.dtype),
        jax.ShapeDtypeStruct((nwalk, GLA_VAL_DIM, GLA_DK), F32),
    ]
    o, st = pl.pallas_call(
        functools.partial(_gla_kernel, C=C, L=L, nsub=nsub), grid=(nwalk, nc), in_specs=in_specs, out_specs=out_specs,
        out_shape=out_shape, scratch_shapes=[pltpu.VMEM((GLA_VAL_DIM, GLA_DK), F32)],
        input_output_aliases={9: 0},
        compiler_params=_params(("arbitrary", "arbitrary")))(qkvg, qkvg, qkvg, qkvg, lr, w2, b2, nw, s0_t, o_prev)
    return o, st[:nseq]


def _layer_norm(x, g, b):
    mu = jnp.mean(x, axis=-1, keepdims=True)
    xc = x - mu
    var = jnp.mean(xc * xc, axis=-1, keepdims=True)
    return xc * lax.rsqrt(var + NORM_EPS) * g + b


def _post_kernel(y_ref, wo_ref, h_ref, g_ref, b_ref, wr_ref, br_ref,
                 h1_ref, h1b_ref, ri_ref, rw_ref, cnt_ref, carry_ref, *, alpha):
    i = pl.program_id(0)
    tm = h_ref.shape[0]

    @pl.when(i == 0)
    def _():
        carry_ref[...] = jnp.zeros_like(carry_ref)

    mix = jnp.dot(y_ref[...], wo_ref[...], preferred_element_type=F32)
    h1 = _layer_norm(alpha * h_ref[...] + mix, g_ref[...], b_ref[...])
    h1_ref[...] = h1
    h_hi = h1.astype(BF16)
    h1b_ref[...] = h_hi
    logits = jnp.dot(h_hi, wr_ref[...], preferred_element_type=F32) + br_ref[...]
    lane = _iota((tm, LANES), 1)
    neg = -jnp.inf
    is_g = lane < MOE_GROUPS
    gl = jnp.where(is_g, logits, neg)
    gmax = jnp.max(gl, axis=-1, keepdims=True)
    gsel = jnp.min(jnp.where(gl == gmax, lane, LANES), axis=-1, keepdims=True)
    p_g = 1.0 / jnp.sum(jnp.where(is_g, jnp.exp(gl - gmax), 0.0), axis=-1, keepdims=True)
    eid = lane - MOE_GROUPS
    in_g = (eid >= 0) & (eid < MOE_EXPERTS) & ((eid // MOE_EXP_PER_GROUP) == gsel)
    el = jnp.where(in_g, logits, neg)
    v1 = jnp.max(el, axis=-1, keepdims=True)
    i1 = jnp.min(jnp.where(el == v1, lane, LANES), axis=-1, keepdims=True)
    el2 = jnp.where(lane == i1, neg, el)
    v2 = jnp.max(el2, axis=-1, keepdims=True)
    i2 = jnp.min(jnp.where(el2 == v2, lane, LANES), axis=-1, keepdims=True)
    t = jnp.exp(v2 - v1)
    w1 = p_g / (1.0 + t)
    w2 = p_g * t / (1.0 + t)
    e1 = i1 - MOE_GROUPS
    e2 = i2 - MOE_GROUPS
    oh = jnp.where(lane == e1, 1.0, 0.0) + jnp.where(lane == e2, 1.0, 0.0)
    strict = (_iota((tm, tm), 1) < _iota((tm, tm), 0)).astype(BF16)
    before = jnp.dot(strict, oh.astype(BF16), preferred_element_type=F32) + carry_ref[...]
    r1 = jnp.sum(jnp.where(lane == e1, before, 0.0), axis=-1, keepdims=True)
    r2 = jnp.sum(jnp.where(lane == e2, before, 0.0), axis=-1, keepdims=True)
    carry_ref[...] = carry_ref[...] + jnp.sum(oh, axis=0, keepdims=True)
    cnt_ref[...] = carry_ref[...]
    packed = jnp.where(lane == 0, e1.astype(F32), jnp.where(lane == 1, e2.astype(F32), jnp.where(
        lane == 2, r1, jnp.where(lane == 3, r2, 0.0))))
    ri_ref[...] = packed.T[0:8, :]
    rw_ref[...] = jnp.where(lane == 0, w1, jnp.where(lane == 1, w2, 0.0))


def _post_mixer(y, wo, h, g, b, wr, br, *, alpha):
    rows, kin = y.shape
    grid = (rows // POST_TILE,)
    rmap = lambda i: (i, 0)
    const = lambda i: (0, 0)
    in_specs = [
        pl.BlockSpec((POST_TILE, kin), rmap),
        pl.BlockSpec((kin, D_MODEL), const),
        pl.BlockSpec((POST_TILE, D_MODEL), rmap),
        pl.BlockSpec((1, D_MODEL), const),
        pl.BlockSpec((1, D_MODEL), const),
        pl.BlockSpec((D_MODEL, LANES), const),
        pl.BlockSpec((1, LANES), const),
    ]
    out_specs = [
        pl.BlockSpec((POST_TILE, D_MODEL), rmap),
        pl.BlockSpec((POST_TILE, D_MODEL), rmap),
        pl.BlockSpec((8, POST_TILE), lambda i: (0, i)),
        pl.BlockSpec((POST_TILE, LANES), rmap),
        pl.BlockSpec((1, LANES), const),
    ]
    out_shape = [
        jax.ShapeDtypeStruct((rows, D_MODEL), F32),
        jax.ShapeDtypeStruct((rows, D_MODEL), BF16),
        jax.ShapeDtypeStruct((8, rows), F32),
        jax.ShapeDtypeStruct((rows, LANES), F32),
        jax.ShapeDtypeStruct((1, LANES), F32),
    ]
    return pl.pallas_call(
        functools.partial(_post_kernel, alpha=alpha), grid=grid, in_specs=in_specs, out_specs=out_specs,
        out_shape=out_shape, scratch_shapes=[pltpu.VMEM((1, LANES), F32)],
        compiler_params=_params(("arbitrary",)))(y, wo, h, g, b, wr, br)


def _expert_kernel(be_ref, nu_ref, x_ref, *refs):
    nb = EXPERT_STEP_BLOCKS
    w_refs, o_ref, (wg_b, wu_b, wd_b) = refs[:3 * nb], refs[3 * nb], refs[3 * nb + 1:]
    i = pl.program_id(0)
    for half in range(nb):
        blk = i * nb + half
        wg_ref, wu_ref, wd_ref = w_refs[3 * half:3 * half + 3]
        rs = slice(half * EXPERT_BLOCK, (half + 1) * EXPERT_BLOCK)
        prev = be_ref[jnp.maximum(blk - 1, 0)]

        @pl.when(jnp.logical_or(blk == 0, be_ref[blk] != prev))
        def _():
            wg_b[...] = wg_ref[...].astype(BF16)
            wu_b[...] = wu_ref[...].astype(BF16)
            wd_b[...] = wd_ref[...].astype(BF16)

        @pl.when(blk < nu_ref[0])
        def _():
            x = x_ref[rs, :]
            hg = jnp.dot(x, wg_b[...], preferred_element_type=F32)
            hu = jnp.dot(x, wu_b[...], preferred_element_type=F32)
            hh = (hg * _sigmoid(hg) * hu).astype(BF16)
            o_ref[rs, :] = jnp.dot(hh, wd_b[...], preferred_element_type=F32).astype(o_ref.dtype)

        @pl.when(blk >= nu_ref[0])
        def _():
            o_ref[rs, :] = jnp.zeros((EXPERT_BLOCK, D_MODEL), o_ref.dtype)


def _experts(block_e, n_used, xb, w_gate, w_up, w_down, layer):
    nb = EXPERT_STEP_BLOCKS
    step_rows = nb * EXPERT_BLOCK
    nstep = xb.shape[0] // step_rows
    w_specs, w_args = [], []
    for half in range(nb):
        wmap = lambda i, be, nu, half=half: (layer, be[i * nb + half], 0, 0)
        w_specs += [pl.BlockSpec((None, None, D_MODEL, EXPERT_FF), wmap),
                    pl.BlockSpec((None, None, D_MODEL, EXPERT_FF), wmap),
                    pl.BlockSpec((None, None, EXPERT_FF, D_MODEL), wmap)]
        w_args += [w_gate, w_up, w_down]
    grid_spec = pltpu.PrefetchScalarGridSpec(
        num_scalar_prefetch=2, grid=(nstep,),
        in_specs=[pl.BlockSpec((step_rows, D_MODEL), lambda i, be, nu: (i, 0))] + w_specs,
        out_specs=pl.BlockSpec((step_rows, D_MODEL), lambda i, be, nu: (i, 0)),
        scratch_shapes=[pltpu.VMEM((D_MODEL, EXPERT_FF), BF16), pltpu.VMEM((D_MODEL, EXPERT_FF), BF16),
                        pltpu.VMEM((EXPERT_FF, D_MODEL), BF16)])
    return pl.pallas_call(
        _expert_kernel, grid_spec=grid_spec,
        out_shape=jax.ShapeDtypeStruct(xb.shape, BF16),
        compiler_params=_params(("arbitrary",)))(block_e, n_used, xb, *w_args)


def _combine_kernel(h_ref, ya_ref, yb_ref, rw_ref, g_ref, b_ref, o_ref, *, alpha):
    rw = rw_ref[...]
    ffn = ya_ref[...].astype(F32) * rw[:, 0:1] + yb_ref[...].astype(F32) * rw[:, 1:2]
    o_ref[...] = _layer_norm(alpha * h_ref[...] + ffn, g_ref[...], b_ref[...])


def _combine(h1, ya, yb, rw, g, b, *, alpha):
    rows = h1.shape[0]
    rmap = lambda i: (i, 0)
    const = lambda i: (0, 0)
    in_specs = [
        pl.BlockSpec((POST_TILE, D_MODEL), rmap), pl.BlockSpec((POST_TILE, D_MODEL), rmap),
        pl.BlockSpec((POST_TILE, D_MODEL), rmap), pl.BlockSpec((POST_TILE, LANES), rmap),
        pl.BlockSpec((1, D_MODEL), const), pl.BlockSpec((1, D_MODEL), const),
    ]
    return pl.pallas_call(
        functools.partial(_combine_kernel, alpha=alpha), grid=(rows // POST_TILE,), in_specs=in_specs,
        out_specs=pl.BlockSpec((POST_TILE, D_MODEL), rmap),
        out_shape=jax.ShapeDtypeStruct((rows, D_MODEL), F32),
        compiler_params=_params(("parallel",)))(h1, ya, yb, rw, g, b)


def _extract_kernel(h_ref, o_ref, sems, *, starts, n):
    copies = [pltpu.make_async_copy(h_ref.at[pl.ds(s, n)], o_ref.at[b], sems.at[b]) for b, s in enumerate(starts)]
    for cp in copies:
        cp.start()
    for cp in copies:
        cp.wait()


def _extract_rows(h, starts, n):
    any_spec = pl.BlockSpec(memory_space=pl.ANY)
    return pl.pallas_call(
        functools.partial(_extract_kernel, starts=tuple(starts), n=n), in_specs=[any_spec], out_specs=any_spec,
        out_shape=jax.ShapeDtypeStruct((len(starts), n, h.shape[1]), h.dtype),
        scratch_shapes=[pltpu.SemaphoreType.DMA((len(starts),))])(h)


def _round_up(x, m):
    return (x + m - 1) // m * m


def _pad_cols(w, n):
    return jnp.pad(w, ((0, 0), (0, n - w.shape[1])))


def _take_rows(x, idx):
    return x.at[idx].get(mode='promise_in_bounds')


def _invert_kernel(npair_ref, dest_ref, init_ref, out_ref, inv_smem, buf0, buf1, sems, *, n_tok):
    nch = dest_ref.shape[0]
    bufs = (buf0, buf1)

    def chunk_copy(ch, slot):
        return pltpu.make_async_copy(dest_ref.at[ch], bufs[slot], sems.at[slot])

    init_copy = pltpu.make_async_copy(init_ref, inv_smem, sems.at[2])
    init_copy.start()
    chunk_copy(0, 0).start()
    init_copy.wait()

    def per_pair(pair, carry):
        for slot in range(2):
            ch = 2 * pair + slot
            chunk_copy(ch, slot).wait()

            @pl.when(ch + 1 < nch)
            def _():
                chunk_copy(ch + 1, 1 - slot).start()

            flat0 = ch * INV_CHUNK
            tok0 = jnp.where(flat0 >= n_tok, flat0 - n_tok, flat0)
            buf = bufs[slot]

            def per_slot(j, inner):
                inv_smem[buf[j]] = tok0 + j
                return inner

            lax.fori_loop(0, INV_CHUNK, per_slot, 0, unroll=16)
        return carry

    lax.fori_loop(0, npair_ref[0], per_pair, 0)
    out_copy = pltpu.make_async_copy(inv_smem, out_ref, sems.at[2])
    out_copy.start()
    out_copy.wait()


def _invert_slots(dest, init):
    n_tok = dest.shape[1]
    assert n_tok % INV_CHUNK == 0 and init.shape[0] % INV_CHUNK == 0
    any_spec = pl.BlockSpec(memory_space=pl.ANY)
    npair = jnp.full((1,), 2 * n_tok // INV_CHUNK // 2, I32)
    grid_spec = pltpu.PrefetchScalarGridSpec(
        num_scalar_prefetch=1, grid=(), in_specs=[any_spec, any_spec], out_specs=any_spec,
        scratch_shapes=[pltpu.SMEM(init.shape, I32), pltpu.SMEM((INV_CHUNK,), I32), pltpu.SMEM((INV_CHUNK,), I32),
                        pltpu.SemaphoreType.DMA((3,))])
    return pl.pallas_call(
        functools.partial(_invert_kernel, n_tok=n_tok), grid_spec=grid_spec,
        out_shape=jax.ShapeDtypeStruct(init.shape, I32),
    )(npair, dest.reshape(-1, INV_CHUNK), init)


def _moe(h1b, ri, cnt, w_gate, w_up, w_down, layer):
    rows = h1b.shape[0]
    nblk = _round_up((2 * rows + MOE_EXPERTS * (EXPERT_BLOCK - 1) + EXPERT_BLOCK - 1) // EXPERT_BLOCK,
                     EXPERT_STEP_BLOCKS)
    counts = cnt[0, :MOE_EXPERTS].astype(I32)
    pcounts = (counts + EXPERT_BLOCK - 1) // EXPERT_BLOCK * EXPERT_BLOCK
    pends = jnp.cumsum(pcounts)
    pstarts = pends - pcounts
    ri = ri.astype(I32)
    dest = _take_rows(pstarts, ri[0:2].reshape(-1)).reshape(2, rows) + ri[2:4]
    nslot = _round_up(nblk * EXPERT_BLOCK, INV_CHUNK)
    row_tok = _invert_slots(dest, jnp.arange(nslot, dtype=I32) % rows)[:nblk * EXPERT_BLOCK]
    blk_start = jnp.arange(nblk, dtype=I32) * EXPERT_BLOCK
    block_e = jnp.minimum(jnp.sum((pends[None, :] <= blk_start[:, None]).astype(I32), axis=1), MOE_EXPERTS - 1)
    n_used = (pends[-1:] // EXPERT_BLOCK).astype(I32)
    xb = _take_rows(h1b, row_tok)
    yb = _experts(block_e, n_used, xb, w_gate, w_up, w_down, layer)
    return _take_rows(yb, dest[0]), _take_rows(yb, dest[1])


def kernel(x_prompt, x_sample, state_ssd, state_ssd_conv, state_gla, meta_tokens, ssd_w_in, ssd_conv_w, ssd_conv_b, ssd_dt_bias, ssd_A_log, ssd_D, ssd_norm_w, ssd_w_out, gla_w_in, gla_w_gk2, gla_b_gk2, gla_norm_w, gla_w_out, ln1_g, ln1_b, moe_w_grp, moe_b_grp, moe_w_exp, moe_b_exp, moe_w_gate, moe_w_up, moe_w_down, ln2_g, ln2_b):
    bp, seq, _ = x_prompt.shape
    bs, lsm, _ = x_sample.shape
    depth = ln1_g.shape[0]
    alpha = (2.0 * depth) ** 0.25
    lp = N_META + seq
    cp = PROMPT_CHUNK
    lp_pad = _round_up(lp, PROMPT_STEP_CHUNKS * cp)
    cs = lsm
    off_s = bp * lp_pad
    assert off_s % cs == 0 and cs % GLA_SUB == 0 and cp % GLA_SUB == 0
    rows = _round_up(off_s + bs * lsm, max(POST_TILE, INV_CHUNK))

    pieces = []
    for b in range(bp):
        pieces += [meta_tokens.astype(F32), x_prompt[b], jnp.zeros((lp_pad - lp, D_MODEL), F32)]
    pieces += [x_sample.reshape(bs * lsm, D_MODEL), jnp.zeros((rows - off_s - bs * lsm, D_MODEL), F32)]
    h = jnp.concatenate(pieces, axis=0)

    groups = (dict(row_off=0, nseq=bp, nwalk=bp, L=lp, Lpad=lp_pad, C=cp),
              dict(row_off=off_s, nseq=bs, nwalk=(rows - off_s) // cs, L=lsm, Lpad=lsm, C=cs))
    ymix_ssd = jnp.zeros((rows, SSD_INNER), BF16)
    ymix_gla = jnp.zeros((rows, GLA_VAL_DIM), BF16)

    new_ssd_p, new_ssd_s, new_conv_p, new_conv_s, new_gla_p, new_gla_s = [], [], [], [], [], []
    for i in range(depth):
        j = i // 2
        if i % 2 == 0:
            w_in = ssd_w_in[j].astype(BF16)
            z, xbc, dtr = _proj(
                h, [w_in[:, :SSD_INNER], w_in[:, SSD_INNER:SSD_INNER + SSD_CONV_DIM],
                    _pad_cols(w_in[:, SSD_INNER + SSD_CONV_DIM:], LANES)], [F32, F32, F32])
            cw = ssd_conv_w[j]
            cb = ssd_conv_b[j][None]
            dtb = _pad_cols(ssd_dt_bias[j][None], LANES)
            an = _pad_cols(-jnp.exp(ssd_A_log[j].astype(F32))[None], LANES)
            dsk = jnp.repeat(ssd_D[j], SSD_HEAD_DIM)[None]
            nw = ssd_norm_w[j][None]
            s0s = [jnp.zeros((bp, SSD_STATE, SSD_INNER), F32),
                   jnp.swapaxes(state_ssd[j].reshape(bs, SSD_INNER, SSD_STATE), 1, 2)]
            c0s = [jnp.zeros((bp, 8, SSD_CONV_DIM), F32),
                   jnp.pad(state_ssd_conv[j], ((0, 0), (8 - (SSD_CONV_W - 1), 0), (0, 0)))]
            for grp, s0, c0, acc_s, acc_c in zip(groups, s0s, c0s, (new_ssd_p, new_ssd_s), (new_conv_p, new_conv_s)):
                ymix_ssd, st = _ssd_scan(z, xbc, dtr, cw, cb, dtb, an, dsk, nw, s0, c0, ymix_ssd, **grp)
                n, L, Lpad = grp['nseq'], grp['L'], grp['Lpad']
                acc_s.append(jnp.swapaxes(st, 1, 2).reshape(n, SSD_HEADS, SSD_HEAD_DIM, SSD_STATE))
                nconv = SSD_CONV_W - 1
                last = (grp['row_off'] + jnp.arange(n, dtype=I32)[:, None] * Lpad + (L - nconv)
                        + jnp.arange(nconv, dtype=I32)[None, :])
                acc_c.append(_take_rows(xbc, last.reshape(-1)).reshape(n, nconv, SSD_CONV_DIM))
            ymix = ymix_ssd
            wo = ssd_w_out[j].astype(BF16)
        else:
            w_in = gla_w_in[j].astype(BF16)
            nq = 2 * GLA_KEY_DIM + 2 * GLA_VAL_DIM
            qkvg, lr = _proj(h, [w_in[:, :nq], _pad_cols(w_in[:, nq:], LANES)], [F32, F32])
            w2 = jnp.pad(gla_w_gk2[j], ((0, LANES - GLA_GATE_RANK), (0, 0))).astype(BF16)
            b2 = gla_b_gk2[j][None]
            nw = gla_norm_w[j][None]
            s0s = [jnp.zeros((bp, GLA_VAL_DIM, GLA_DK), F32),
                   jnp.swapaxes(state_gla[j], 2, 3).reshape(bs, GLA_VAL_DIM, GLA_DK)]
            for grp, s0, acc, nsub in zip(groups, s0s, (new_gla_p, new_gla_s), (PROMPT_STEP_CHUNKS, 1)):
                ymix_gla, st = _gla_scan(qkvg, lr, w2, b2, nw, s0, ymix_gla, nsub=nsub, **grp)
                acc.append(jnp.swapaxes(st.reshape(grp['nseq'], GLA_HEADS, GLA_DV, GLA_DK), 2, 3))
            ymix = ymix_gla
            wo = gla_w_out[j].astype(BF16)

        wr = _pad_cols(jnp.concatenate([moe_w_grp[i], moe_w_exp[i]], axis=1), LANES).astype(BF16)
        br = _pad_cols(jnp.concatenate([moe_b_grp[i], moe_b_exp[i]])[None], LANES)
        h1, h1b, ri, rw, cnt = _post_mixer(ymix, wo, h, ln1_g[i][None], ln1_b[i][None], wr, br, alpha=alpha)
        ya, yb = _moe(h1b, ri, cnt, moe_w_gate, moe_w_up, moe_w_down, i)
        h = _combine(h1, ya, yb, rw, ln2_g[i][None], ln2_b[i][None], alpha=alpha)

    y_prompt = _extract_rows(h, [b * lp_pad + N_META for b in range(bp)], seq)
    y_sample = h[off_s:off_s + bs * lsm].reshape(bs, lsm, D_MODEL)
    return (y_prompt, y_sample, jnp.stack(new_ssd_p), jnp.stack(new_conv_p), jnp.stack(new_gla_p),
            jnp.stack(new_ssd_s), jnp.stack(new_conv_s), jnp.stack(new_gla_s))
```

```python
import functools

import jax
import jax.numpy as jnp
from jax import lax
from jax.experimental import pallas as pl
from jax.experimental.pallas import tpu as pltpu

F32 = jnp.float32
BF16 = jnp.bfloat16
I32 = jnp.int32
NT = (((1,), (1,)), ((), ()))
TN = (((0,), (0,)), ((), ()))
NN = (((1,), (0,)), ((), ()))

D_MODEL = 1024
N_META = 16
SSD_INNER = 2048
SSD_HEAD_DIM = 64
SSD_HEADS = 32
SSD_GROUPS = 4
SSD_STATE = 128
SSD_CONV_W = 4
SSD_CONV_DIM = 3072
GLA_HEADS = 4
GLA_KEY_DIM = 512
GLA_VAL_DIM = 1024
GLA_DK = 128
GLA_DV = 256
GLA_GATE_RANK = 16
GLA_GATE_NORM = 16.0
MOE_GROUPS = 4
MOE_EXP_PER_GROUP = 8
MOE_EXPERTS = 32
EXPERT_FF = 512
NORM_EPS = 1e-5

LANES = 128
VMEM_LIMIT_MB = 56
ROW_TILE = 512
POST_TILE = 1024
PROMPT_CHUNK = 128
PROMPT_STEP_CHUNKS = 2
GLA_SUB = 8
EXPERT_BLOCK = 256
EXPERT_STEP_BLOCKS = 2
INV_CHUNK = 1024


def _params(sem):
    return pltpu.CompilerParams(dimension_semantics=sem, vmem_limit_bytes=VMEM_LIMIT_MB << 20)


def _sigmoid(x):
    return 1.0 / (1.0 + jnp.exp(-x))


def _softplus(x):
    return jnp.maximum(x, 0.0) + jnp.log1p(jnp.exp(-jnp.abs(x)))


def _iota(shape, dim):
    return lax.broadcasted_iota(I32, shape, dim)


def _exact_dot(a_b, x, dims):
    hi = x.astype(BF16)
    r1 = x - hi.astype(F32)
    mid = r1.astype(BF16)
    lo = (r1 - mid.astype(F32)).astype(BF16)
    f = lambda t: lax.dot_general(a_b, t, dims, preferred_element_type=F32)
    return (f(hi) + f(mid)) + f(lo)


def _proj_kernel(x_ref, *refs):
    n = len(refs) // 2
    xb = x_ref[...].astype(BF16)
    for w_ref, o_ref in zip(refs[:n], refs[n:]):
        o_ref[...] = jnp.dot(xb, w_ref[...], preferred_element_type=F32).astype(o_ref.dtype)


def _proj(x, ws, out_dtypes):
    rows, k = x.shape
    grid = (rows // ROW_TILE,)
    in_specs = [pl.BlockSpec((ROW_TILE, k), lambda i: (i, 0))]
    in_specs += [pl.BlockSpec(w.shape, lambda i: (0, 0)) for w in ws]
    out_specs = [pl.BlockSpec((ROW_TILE, w.shape[1]), lambda i: (i, 0)) for w in ws]
    out_shape = [jax.ShapeDtypeStruct((rows, w.shape[1]), dt) for w, dt in zip(ws, out_dtypes)]
    return pl.pallas_call(
        _proj_kernel, grid=grid, in_specs=in_specs, out_specs=out_specs, out_shape=out_shape,
        compiler_params=_params(("parallel",)))(x, *ws)


def _ssd_kernel(z_ref, xbc_ref, dt_ref, cw_ref, cb_ref, dtb_ref, an_ref, dsk_ref, nw_ref, s0_ref, c0_ref,
                yprev_ref, y_ref, sout_ref, st_ref, xc_ref, u_ref, *, C, L):
    del yprev_ref
    c = pl.program_id(1)
    hist = 8

    @pl.when(c == 0)
    def _():
        st_ref[...] = s0_ref[0]
        xc_ref[0:hist, :] = c0_ref[0]

    @pl.when(c > 0)
    def _():
        xc_ref[0:hist, :] = xc_ref[C:C + hist, :]

    xc_ref[hist:hist + C, :] = xbc_ref[...].astype(F32)

    for j in range(0, SSD_CONV_DIM, 512):
        sl = slice(j, j + 512)
        acc = cb_ref[:, sl] + xc_ref[hist:hist + C, sl] * cw_ref[3:4, sl]
        for w in range(SSD_CONV_W - 1):
            off = hist - (SSD_CONV_W - 1) + w
            acc = acc + xc_ref[off:off + C, sl] * cw_ref[w:w + 1, sl]
        u_ref[:, sl] = acc * _sigmoid(acc)

    row = _iota((C, LANES), 0) + c * C
    dt = jnp.where(row < L, _softplus(dt_ref[...] + dtb_ref[...]), 0.0)
    loga = dt * an_ref[...]
    ti = _iota((C, C), 0)
    si = _iota((C, C), 1)
    tril = si <= ti
    eye_b = (_iota((LANES, LANES), 0) == _iota((LANES, LANES), 1)).astype(BF16)
    cum = _exact_dot(tril.astype(BF16), loga, NN)
    cum_t = _exact_dot(eye_b, cum, NT)
    dt_t = _exact_dot(eye_b, dt, NT)
    w_t = dt_t * jnp.exp(cum_t[:, C - 1:C] - cum_t)
    src_t = cum_t - jnp.log(dt_t)
    e_last = jnp.exp(cum[C - 1:C, :])
    lane = _iota((C, LANES), 1)
    lane_s = _iota((SSD_STATE, LANES), 1)
    lane_r = _iota((1, LANES), 1)
    hpg = SSD_HEADS // SSD_GROUPS

    for g in range(SSD_GROUPS):
        b_f = u_ref[:, SSD_INNER + g * SSD_STATE:SSD_INNER + (g + 1) * SSD_STATE]
        c_f = u_ref[:, SSD_INNER + (SSD_GROUPS + g) * SSD_STATE:SSD_INNER + (SSD_GROUPS + g + 1) * SSD_STATE]
        b_b = b_f.astype(BF16)
        cb = lax.dot_general(c_f.astype(BF16), b_b, NT, preferred_element_type=F32)
        b_t = lax.dot_general(eye_b, b_b, NT, preferred_element_type=F32)
        ypairs = []
        for jp in range(hpg // 2):
            p = g * (hpg // 2) + jp
            cs = slice(p * LANES, (p + 1) * LANES)
            xs_p = u_ref[:, cs]
            xs_b = xs_p.astype(BF16)
            st_p = st_ref[:, cs]
            st_b = st_p.astype(BF16)
            res, upd = [], []
            for h in (2 * p, 2 * p + 1):
                ccol = cum[:, h:h + 1]
                dec = jnp.where(tril, jnp.exp(ccol - src_t[h:h + 1, :]), 0.0)
                wmat = (cb * dec).astype(BF16)
                c_e = (c_f * jnp.exp(ccol)).astype(BF16)
                res.append(jnp.dot(wmat, xs_b, preferred_element_type=F32)
                           + jnp.dot(c_e, st_b, preferred_element_type=F32))
                b_w = (b_t * w_t[h:h + 1, :]).astype(BF16)
                upd.append(jnp.dot(b_w, xs_b, preferred_element_type=F32))
            y_p = jnp.where(lane < SSD_HEAD_DIM, res[0], res[1]) + dsk_ref[:, cs] * xs_p
            e_p = jnp.where(lane_r < SSD_HEAD_DIM, e_last[:, 2 * p:2 * p + 1], e_last[:, 2 * p + 1:2 * p + 2])
            st_ref[:, cs] = st_p * e_p + jnp.where(lane_s < SSD_HEAD_DIM, upd[0], upd[1])
            ypairs.append(y_p)
        gs = slice(g * 512, (g + 1) * 512)
        yg = jnp.concatenate(ypairs, axis=1)
        zg = z_ref[:, gs].astype(F32)
        yg = yg * (zg * _sigmoid(zg))
        ms = jnp.mean(yg * yg, axis=-1, keepdims=True)
        y_ref[:, gs] = (yg * lax.rsqrt(ms + NORM_EPS) * nw_ref[:, gs]).astype(y_ref.dtype)

    @pl.when(c == pl.num_programs(1) - 1)
    def _():
        sout_ref[0] = st_ref[...]


def _ssd_scan(z, xbc, dt, cw, cb, dtb, an, dsk, nw, s0_t, c0, y_prev, *, row_off, nseq, nwalk, L, Lpad, C):
    nc = Lpad // C
    base = row_off // C

    def rmap(b, c):
        return (base + b * nc + c, 0)

    const = lambda b, c: (0, 0)
    in_specs = [
        pl.BlockSpec((C, SSD_INNER), rmap),
        pl.BlockSpec((C, SSD_CONV_DIM), rmap),
        pl.BlockSpec((C, LANES), rmap),
        pl.BlockSpec((SSD_CONV_W, SSD_CONV_DIM), const),
        pl.BlockSpec((1, SSD_CONV_DIM), const),
        pl.BlockSpec((1, LANES), const),
        pl.BlockSpec((1, LANES), const),
        pl.BlockSpec((1, SSD_INNER), const),
        pl.BlockSpec((1, SSD_INNER), const),
        pl.BlockSpec((1, SSD_STATE, SSD_INNER), lambda b, c: (jnp.minimum(b, nseq - 1), 0, 0)),
        pl.BlockSpec((1, 8, SSD_CONV_DIM), lambda b, c: (jnp.minimum(b, nseq - 1), 0, 0)),
        pl.BlockSpec(memory_space=pl.ANY),
    ]
    out_specs = [
        pl.BlockSpec((C, SSD_INNER), rmap),
        pl.BlockSpec((1, SSD_STATE, SSD_INNER), lambda b, c: (b, 0, 0)),
    ]
    out_shape = [
        jax.ShapeDtypeStruct(y_prev.shape, y_prev.dtype),
        jax.ShapeDtypeStruct((nwalk, SSD_STATE, SSD_INNER), F32),
    ]
    scratch = [
        pltpu.VMEM((SSD_STATE, SSD_INNER), F32),
        pltpu.VMEM((C + 8, SSD_CONV_DIM), F32),
        pltpu.VMEM((C, SSD_CONV_DIM), F32),
    ]
    y, st = pl.pallas_call(
        functools.partial(_ssd_kernel, C=C, L=L), grid=(nwalk, nc), in_specs=in_specs, out_specs=out_specs,
        out_shape=out_shape, scratch_shapes=scratch, input_output_aliases={11: 0},
        compiler_params=_params(("arbitrary", "arbitrary")))(z, xbc, dt, cw, cb, dtb, an, dsk, nw, s0_t, c0, y_prev)
    return y, st[:nseq]


def _gla_kernel(q_ref, k_ref, v_ref, g_ref, lr_ref, w2_ref, b2_ref, nw_ref, s0_ref,
                oprev_ref, o_ref, sout_ref, st_ref, *, C, L, nsub):
    del oprev_ref
    c = pl.program_id(1)
    sub = GLA_SUB
    nb = C // sub

    @pl.when(c == 0)
    def _():
        st_ref[...] = s0_ref[0]

    ti = _iota((C, C), 0)
    si = _iota((C, C), 1)
    tril = si <= ti
    tril_b = tril.astype(BF16)
    diag_mask = jnp.logical_and(tril, (ti // sub) == (si // sub))
    lane_o = si % sub
    scale = GLA_DK ** -0.5

    intra = {}
    for s in range(nsub):
        rs = slice(s * C, (s + 1) * C)
        first_row = (c * nsub + s) * C
        gk = jnp.dot(lr_ref[rs, :].astype(BF16), w2_ref[...], preferred_element_type=F32) + b2_ref[...]
        logg = (jnp.minimum(gk, 0.0) - jnp.log1p(jnp.exp(-jnp.abs(gk)))) * (1.0 / GLA_GATE_NORM)
        logg = jnp.where(_iota((C, GLA_KEY_DIM), 0) + first_row < L, logg, 0.0)
        cum_all = _exact_dot(tril_b, logg, NN)
        valid_v = _iota((C, GLA_DV), 0) + first_row < L
        for h in range(GLA_HEADS):
            ks = slice(h * GLA_DK, (h + 1) * GLA_DK)
            vs = slice(h * GLA_DV, (h + 1) * GLA_DV)
            q = q_ref[rs, ks].astype(F32) * scale
            k = k_ref[rs, ks].astype(F32)
            v = jnp.where(valid_v, v_ref[rs, vs].astype(F32), 0.0)
            v_b = v.astype(BF16)
            cum = cum_all[:, ks]
            if nb > 1:
                q_parts, k_parts = [], []
                for i in range(1, nb):
                    r_i = cum[i * sub - 1:i * sub, :]
                    qi = q[i * sub:(i + 1) * sub, :] * jnp.exp(cum[i * sub:(i + 1) * sub, :] - r_i)
                    pieces = [jnp.zeros((i * sub, GLA_DK), F32), qi]
                    if C - (i + 1) * sub > 0:
                        pieces.append(jnp.zeros((C - (i + 1) * sub, GLA_DK), F32))
                    q_parts.append(jnp.concatenate(pieces, axis=0))
                    ki = k[0:i * sub, :] * jnp.exp(r_i - cum[0:i * sub, :])
                    k_parts.append(jnp.concatenate([ki, jnp.zeros((C - i * sub, GLA_DK), F32)], axis=0))
                q_st = jnp.concatenate(q_parts, axis=1).astype(BF16)
                k_st = jnp.concatenate(k_parts, axis=1).astype(BF16)
                a_off = lax.dot_general(q_st, k_st, NT, preferred_element_type=F32)
            else:
                a_off = jnp.zeros((C, C), F32)
            q3 = q.reshape(nb, sub, GLA_DK)
            k3 = k.reshape(nb, sub, GLA_DK)
            c3 = cum.reshape(nb, sub, GLA_DK)
            a_diag = jnp.zeros((C, C), F32)
            for o in range(sub):
                p_o = q3 * jnp.exp(jnp.minimum(c3 - c3[:, o:o + 1, :], 0.0)) * k3[:, o:o + 1, :]
                a_diag = jnp.where(lane_o == o, jnp.sum(p_o.reshape(C, GLA_DK), axis=-1, keepdims=True), a_diag)
            a = a_off + jnp.where(diag_mask, a_diag, 0.0)
            y_intra = jnp.dot(a.astype(BF16), v_b, preferred_element_type=F32)
            last = cum[C - 1:C, :]
            q_e = (q * jnp.exp(cum)).astype(BF16)
            k_e = (k * jnp.exp(last - cum)).astype(BF16)
            intra[s, h] = (y_intra, q_e, k_e, v_b, jnp.exp(last))

    for h in range(GLA_HEADS):
        vs = slice(h * GLA_DV, (h + 1) * GLA_DV)
        st = st_ref[vs, :]
        for s in range(nsub):
            rs = slice(s * C, (s + 1) * C)
            y_intra, q_e, k_e, v_b, e_last = intra[s, h]
            y = y_intra + lax.dot_general(q_e, st.astype(BF16), NT, preferred_element_type=F32)
            st = st * e_last + lax.dot_general(v_b, k_e, TN, preferred_element_type=F32)
            ms = jnp.mean(y * y, axis=-1, keepdims=True)
            gate = g_ref[rs, vs].astype(F32)
            o_ref[rs, vs] = (y * lax.rsqrt(ms + NORM_EPS) * nw_ref[...]
                             * (gate * _sigmoid(gate))).astype(o_ref.dtype)
        st_ref[vs, :] = st

    @pl.when(c == pl.num_programs(1) - 1)
    def _():
        sout_ref[0] = st_ref[...]


def _gla_scan(qkvg, lr, w2, b2, nw, s0_t, o_prev, *, row_off, nseq, nwalk, L, Lpad, C, nsub):
    blk = nsub * C
    nc = Lpad // blk
    base = row_off // blk
    const = lambda b, c: (0, 0)
    in_specs = [
        pl.BlockSpec((blk, GLA_KEY_DIM), lambda b, c: (base + b * nc + c, 0)),
        pl.BlockSpec((blk, GLA_KEY_DIM), lambda b, c: (base + b * nc + c, 1)),
        pl.BlockSpec((blk, GLA_VAL_DIM), lambda b, c: (base + b * nc + c, 1)),
        pl.BlockSpec((blk, GLA_VAL_DIM), lambda b, c: (base + b * nc + c, 2)),
        pl.BlockSpec((blk, LANES), lambda b, c: (base + b * nc + c, 0)),
        pl.BlockSpec((LANES, GLA_KEY_DIM), const),
        pl.BlockSpec((1, GLA_KEY_DIM), const),
        pl.BlockSpec((1, GLA_DV), const),
        pl.BlockSpec((1, GLA_VAL_DIM, GLA_DK), lambda b, c: (jnp.minimum(b, nseq - 1), 0, 0)),
        pl.BlockSpec(memory_space=pl.ANY),
    ]
    out_specs = [
        pl.BlockSpec((blk, GLA_VAL_DIM), lambda b, c: (base + b * nc + c, 0)),
        pl.BlockSpec((1, GLA_VAL_DIM, GLA_DK), lambda b, c: (b, 0, 0)),
    ]
    out_shape = [
        jax.ShapeDtypeStruct(o_prev.shape, o_prev.dtype),
        jax.ShapeDtypeStruct((nwalk, GLA_VAL_DIM, GLA_DK), F32),
    ]
    o, st = pl.pallas_call(
        functools.partial(_gla_kernel, C=C, L=L, nsub=nsub), grid=(nwalk, nc), in_specs=in_specs, out_specs=out_specs,
        out_shape=out_shape, scratch_shapes=[pltpu.VMEM((GLA_VAL_DIM, GLA_DK), F32)],
        input_output_aliases={9: 0},
        compiler_params=_params(("arbitrary", "arbitrary")))(qkvg, qkvg, qkvg, qkvg, lr, w2, b2, nw, s0_t, o_prev)
    return o, st[:nseq]


def _layer_norm(x, g, b):
    mu = jnp.mean(x, axis=-1, keepdims=True)
    xc = x - mu
    var = jnp.mean(xc * xc, axis=-1, keepdims=True)
    return xc * lax.rsqrt(var + NORM_EPS) * g + b


def _post_kernel(y_ref, wo_ref, h_ref, g_ref, b_ref, wr_ref, br_ref,
                 h1_ref, h1b_ref, ri_ref, rw_ref, cnt_ref, carry_ref, *, alpha):
    i = pl.program_id(0)
    tm = h_ref.shape[0]

    @pl.when(i == 0)
    def _():
        carry_ref[...] = jnp.zeros_like(carry_ref)

    mix = jnp.dot(y_ref[...], wo_ref[...], preferred_element_type=F32)
    h1 = _layer_norm(alpha * h_ref[...] + mix, g_ref[...], b_ref[...])
    h1_ref[...] = h1
    h_hi = h1.astype(BF16)
    h1b_ref[...] = h_hi
    logits = jnp.dot(h_hi, wr_ref[...], preferred_element_type=F32) + br_ref[...]
    lane = _iota((tm, LANES), 1)
    neg = -jnp.inf
    is_g = lane < MOE_GROUPS
    gl = jnp.where(is_g, logits, neg)
    gmax = jnp.max(gl, axis=-1, keepdims=True)
    gsel = jnp.min(jnp.where(gl == gmax, lane, LANES), axis=-1, keepdims=True)
    p_g = 1.0 / jnp.sum(jnp.where(is_g, jnp.exp(gl - gmax), 0.0), axis=-1, keepdims=True)
    eid = lane - MOE_GROUPS
    in_g = (eid >= 0) & (eid < MOE_EXPERTS) & ((eid // MOE_EXP_PER_GROUP) == gsel)
    el = jnp.where(in_g, logits, neg)
    v1 = jnp.max(el, axis=-1, keepdims=True)
    i1 = jnp.min(jnp.where(el == v1, lane, LANES), axis=-1, keepdims=True)
    el2 = jnp.where(lane == i1, neg, el)
    v2 = jnp.max(el2, axis=-1, keepdims=True)
    i2 = jnp.min(jnp.where(el2 == v2, lane, LANES), axis=-1, keepdims=True)
    t = jnp.exp(v2 - v1)
    w1 = p_g / (1.0 + t)
    w2 = p_g * t / (1.0 + t)
    e1 = i1 - MOE_GROUPS
    e2 = i2 - MOE_GROUPS
    oh = jnp.where(lane == e1, 1.0, 0.0) + jnp.where(lane == e2, 1.0, 0.0)
    strict = (_iota((tm, tm), 1) < _iota((tm, tm), 0)).astype(BF16)
    before = jnp.dot(strict, oh.astype(BF16), preferred_element_type=F32) + carry_ref[...]
    r1 = jnp.sum(jnp.where(lane == e1, before, 0.0), axis=-1, keepdims=True)
    r2 = jnp.sum(jnp.where(lane == e2, before, 0.0), axis=-1, keepdims=True)
    carry_ref[...] = carry_ref[...] + jnp.sum(oh, axis=0, keepdims=True)
    cnt_ref[...] = carry_ref[...]
    packed = jnp.where(lane == 0, e1.astype(F32), jnp.where(lane == 1, e2.astype(F32), jnp.where(
        lane == 2, r1, jnp.where(lane == 3, r2, 0.0))))
    ri_ref[...] = packed.T[0:8, :]
    rw_ref[...] = jnp.where(lane == 0, w1, jnp.where(lane == 1, w2, 0.0))


def _post_mixer(y, wo, h, g, b, wr, br, *, alpha):
    rows, kin = y.shape
    grid = (rows // POST_TILE,)
    rmap = lambda i: (i, 0)
    const = lambda i: (0, 0)
    in_specs = [
        pl.BlockSpec((POST_TILE, kin), rmap),
        pl.BlockSpec((kin, D_MODEL), const),
        pl.BlockSpec((POST_TILE, D_MODEL), rmap),
        pl.BlockSpec((1, D_MODEL), const),
        pl.BlockSpec((1, D_MODEL), const),
        pl.BlockSpec((D_MODEL, LANES), const),
        pl.BlockSpec((1, LANES), const),
    ]
    out_specs = [
        pl.BlockSpec((POST_TILE, D_MODEL), rmap),
        pl.BlockSpec((POST_TILE, D_MODEL), rmap),
        pl.BlockSpec((8, POST_TILE), lambda i: (0, i)),
        pl.BlockSpec((POST_TILE, LANES), rmap),
        pl.BlockSpec((1, LANES), const),
    ]
    out_shape = [
        jax.ShapeDtypeStruct((rows, D_MODEL), F32),
        jax.ShapeDtypeStruct((rows, D_MODEL), BF16),
        jax.ShapeDtypeStruct((8, rows), F32),
        jax.ShapeDtypeStruct((rows, LANES), F32),
        jax.ShapeDtypeStruct((1, LANES), F32),
    ]
    return pl.pallas_call(
        functools.partial(_post_kernel, alpha=alpha), grid=grid, in_specs=in_specs, out_specs=out_specs,
        out_shape=out_shape, scratch_shapes=[pltpu.VMEM((1, LANES), F32)],
        compiler_params=_params(("arbitrary",)))(y, wo, h, g, b, wr, br)


def _expert_kernel(be_ref, nu_ref, x_ref, *refs):
    nb = EXPERT_STEP_BLOCKS
    w_refs, o_ref, (wg_b, wu_b, wd_b) = refs[:3 * nb], refs[3 * nb], refs[3 * nb + 1:]
    i = pl.program_id(0)
    for half in range(nb):
        blk = i * nb + half
        wg_ref, wu_ref, wd_ref = w_refs[3 * half:3 * half + 3]
        rs = slice(half * EXPERT_BLOCK, (half + 1) * EXPERT_BLOCK)
        prev = be_ref[jnp.maximum(blk - 1, 0)]

        @pl.when(jnp.logical_or(blk == 0, be_ref[blk] != prev))
        def _():
            wg_b[...] = wg_ref[...].astype(BF16)
            wu_b[...] = wu_ref[...].astype(BF16)
            wd_b[...] = wd_ref[...].astype(BF16)

        @pl.when(blk < nu_ref[0])
        def _():
            x = x_ref[rs, :]
            hg = jnp.dot(x, wg_b[...], preferred_element_type=F32)
            hu = jnp.dot(x, wu_b[...], preferred_element_type=F32)
            hh = (hg * _sigmoid(hg) * hu).astype(BF16)
            o_ref[rs, :] = jnp.dot(hh, wd_b[...], preferred_element_type=F32).astype(o_ref.dtype)

        @pl.when(blk >= nu_ref[0])
        def _():
            o_ref[rs, :] = jnp.zeros((EXPERT_BLOCK, D_MODEL), o_ref.dtype)


def _experts(block_e, n_used, xb, w_gate, w_up, w_down, layer):
    nb = EXPERT_STEP_BLOCKS
    step_rows = nb * EXPERT_BLOCK
    nstep = xb.shape[0] // step_rows
    w_specs, w_args = [], []
    for half in range(nb):
        wmap = lambda i, be, nu, half=half: (layer, be[i * nb + half], 0, 0)
        w_specs += [pl.BlockSpec((None, None, D_MODEL, EXPERT_FF), wmap),
                    pl.BlockSpec((None, None, D_MODEL, EXPERT_FF), wmap),
                    pl.BlockSpec((None, None, EXPERT_FF, D_MODEL), wmap)]
        w_args += [w_gate, w_up, w_down]
    grid_spec = pltpu.PrefetchScalarGridSpec(
        num_scalar_prefetch=2, grid=(nstep,),
        in_specs=[pl.BlockSpec((step_rows, D_MODEL), lambda i, be, nu: (i, 0))] + w_specs,
        out_specs=pl.BlockSpec((step_rows, D_MODEL), lambda i, be, nu: (i, 0)),
        scratch_shapes=[pltpu.VMEM((D_MODEL, EXPERT_FF), BF16), pltpu.VMEM((D_MODEL, EXPERT_FF), BF16),
                        pltpu.VMEM((EXPERT_FF, D_MODEL), BF16)])
    return pl.pallas_call(
        _expert_kernel, grid_spec=grid_spec,
        out_shape=jax.ShapeDtypeStruct(xb.shape, BF16),
        compiler_params=_params(("arbitrary",)))(block_e, n_used, xb, *w_args)


def _combine_kernel(h_ref, ya_ref, yb_ref, rw_ref, g_ref, b_ref, o_ref, *, alpha):
    rw = rw_ref[...]
    ffn = ya_ref[...].astype(F32) * rw[:, 0:1] + yb_ref[...].astype(F32) * rw[:, 1:2]
    o_ref[...] = _layer_norm(alpha * h_ref[...] + ffn, g_ref[...], b_ref[...])


def _combine(h1, ya, yb, rw, g, b, *, alpha):
    rows = h1.shape[0]
    rmap = lambda i: (i, 0)
    const = lambda i: (0, 0)
    in_specs = [
        pl.BlockSpec((POST_TILE, D_MODEL), rmap), pl.BlockSpec((POST_TILE, D_MODEL), rmap),
        pl.BlockSpec((POST_TILE, D_MODEL), rmap), pl.BlockSpec((POST_TILE, LANES), rmap),
        pl.BlockSpec((1, D_MODEL), const), pl.BlockSpec((1, D_MODEL), const),
    ]
    return pl.pallas_call(
        functools.partial(_combine_kernel, alpha=alpha), grid=(rows // POST_TILE,), in_specs=in_specs,
        out_specs=pl.BlockSpec((POST_TILE, D_MODEL), rmap),
        out_shape=jax.ShapeDtypeStruct((rows, D_MODEL), F32),
        compiler_params=_params(("parallel",)))(h1, ya, yb, rw, g, b)


def _round_up(x, m):
    return (x + m - 1) // m * m


def _pad_cols(w, n):
    return jnp.pad(w, ((0, 0), (0, n - w.shape[1])))


def _take_rows(x, idx):
    return x.at[idx].get(mode='promise_in_bounds')


def _invert_kernel(npair_ref, dest_ref, init_ref, out_ref, inv_smem, buf0, buf1, sems, *, n_tok):
    nch = dest_ref.shape[0]
    bufs = (buf0, buf1)

    def chunk_copy(ch, slot):
        return pltpu.make_async_copy(dest_ref.at[ch], bufs[slot], sems.at[slot])

    init_copy = pltpu.make_async_copy(init_ref, inv_smem, sems.at[2])
    init_copy.start()
    chunk_copy(0, 0).start()
    init_copy.wait()

    def per_pair(pair, carry):
        for slot in range(2):
            ch = 2 * pair + slot
            chunk_copy(ch, slot).wait()

            @pl.when(ch + 1 < nch)
            def _():
                chunk_copy(ch + 1, 1 - slot).start()

            flat0 = ch * INV_CHUNK
            tok0 = jnp.where(flat0 >= n_tok, flat0 - n_tok, flat0)
            buf = bufs[slot]

            def per_slot(j, inner):
                inv_smem[buf[j]] = tok0 + j
                return inner

            lax.fori_loop(0, INV_CHUNK, per_slot, 0, unroll=16)
        return carry

    lax.fori_loop(0, npair_ref[0], per_pair, 0)
    out_copy = pltpu.make_async_copy(inv_smem, out_ref, sems.at[2])
    out_copy.start()
    out_copy.wait()


def _invert_slots(dest, init):
    n_tok = dest.shape[1]
    assert n_tok % INV_CHUNK == 0 and init.shape[0] % INV_CHUNK == 0
    any_spec = pl.BlockSpec(memory_space=pl.ANY)
    npair = jnp.full((1,), 2 * n_tok // INV_CHUNK // 2, I32)
    grid_spec = pltpu.PrefetchScalarGridSpec(
        num_scalar_prefetch=1, grid=(), in_specs=[any_spec, any_spec], out_specs=any_spec,
        scratch_shapes=[pltpu.SMEM(init.shape, I32), pltpu.SMEM((INV_CHUNK,), I32), pltpu.SMEM((INV_CHUNK,), I32),
                        pltpu.SemaphoreType.DMA((3,))])
    return pl.pallas_call(
        functools.partial(_invert_kernel, n_tok=n_tok), grid_spec=grid_spec,
        out_shape=jax.ShapeDtypeStruct(init.shape, I32),
    )(npair, dest.reshape(-1, INV_CHUNK), init)


def _moe(h1b, ri, cnt, w_gate, w_up, w_down, layer):
    rows = h1b.shape[0]
    nblk = _round_up((2 * rows + MOE_EXPERTS * (EXPERT_BLOCK - 1) + EXPERT_BLOCK - 1) // EXPERT_BLOCK,
                     EXPERT_STEP_BLOCKS)
    counts = cnt[0, :MOE_EXPERTS].astype(I32)
    pcounts = (counts + EXPERT_BLOCK - 1) // EXPERT_BLOCK * EXPERT_BLOCK
    pends = jnp.cumsum(pcounts)
    pstarts = pends - pcounts
    ri = ri.astype(I32)
    dest = _take_rows(pstarts, ri[0:2].reshape(-1)).reshape(2, rows) + ri[2:4]
    nslot = _round_up(nblk * EXPERT_BLOCK, INV_CHUNK)
    row_tok = _invert_slots(dest, jnp.arange(nslot, dtype=I32) % rows)[:nblk * EXPERT_BLOCK]
    blk_start = jnp.arange(nblk, dtype=I32) * EXPERT_BLOCK
    block_e = jnp.minimum(jnp.sum((pends[None, :] <= blk_start[:, None]).astype(I32), axis=1), MOE_EXPERTS - 1)
    n_used = (pends[-1:] // EXPERT_BLOCK).astype(I32)
    xb = _take_rows(h1b, row_tok)
    yb = _experts(block_e, n_used, xb, w_gate, w_up, w_down, layer)
    return _take_rows(yb, dest[0]), _take_rows(yb, dest[1])


def kernel(x_prompt, x_sample, state_ssd, state_ssd_conv, state_gla, meta_tokens, ssd_w_in, ssd_conv_w, ssd_conv_b, ssd_dt_bias, ssd_A_log, ssd_D, ssd_norm_w, ssd_w_out, gla_w_in, gla_w_gk2, gla_b_gk2, gla_norm_w, gla_w_out, ln1_g, ln1_b, moe_w_grp, moe_b_grp, moe_w_exp, moe_b_exp, moe_w_gate, moe_w_up, moe_w_down, ln2_g, ln2_b):
    bp, seq, _ = x_prompt.shape
    bs, lsm, _ = x_sample.shape
    depth = ln1_g.shape[0]
    alpha = (2.0 * depth) ** 0.25
    lp = N_META + seq
    cp = PROMPT_CHUNK
    lp_pad = _round_up(lp, PROMPT_STEP_CHUNKS * cp)
    cs = lsm
    off_s = bp * lp_pad
    assert off_s % cs == 0 and cs % GLA_SUB == 0 and cp % GLA_SUB == 0
    rows = _round_up(off_s + bs * lsm, max(POST_TILE, INV_CHUNK))

    pieces = []
    for b in range(bp):
        pieces += [meta_tokens.astype(F32), x_prompt[b], jnp.zeros((lp_pad - lp, D_MODEL), F32)]
    pieces += [x_sample.reshape(bs * lsm, D_MODEL), jnp.zeros((rows - off_s - bs * lsm, D_MODEL), F32)]
    h = jnp.concatenate(pieces, axis=0)

    groups = (dict(row_off=0, nseq=bp, nwalk=bp, L=lp, Lpad=lp_pad, C=cp),
              dict(row_off=off_s, nseq=bs, nwalk=(rows - off_s) // cs, L=lsm, Lpad=lsm, C=cs))
    ymix_ssd = jnp.zeros((rows, SSD_INNER), BF16)
    ymix_gla = jnp.zeros((rows, GLA_VAL_DIM), BF16)

    new_ssd_p, new_ssd_s, new_conv_p, new_conv_s, new_gla_p, new_gla_s = [], [], [], [], [], []
    for i in range(depth):
        j = i // 2
        if i % 2 == 0:
            w_in = ssd_w_in[j].astype(BF16)
            z, xbc, dtr = _proj(
                h, [w_in[:, :SSD_INNER], w_in[:, SSD_INNER:SSD_INNER + SSD_CONV_DIM],
                    _pad_cols(w_in[:, SSD_INNER + SSD_CONV_DIM:], LANES)], [F32, F32, F32])
            cw = ssd_conv_w[j]
            cb = ssd_conv_b[j][None]
            dtb = _pad_cols(ssd_dt_bias[j][None], LANES)
            an = _pad_cols(-jnp.exp(ssd_A_log[j].astype(F32))[None], LANES)
            dsk = jnp.repeat(ssd_D[j], SSD_HEAD_DIM)[None]
            nw = ssd_norm_w[j][None]
            s0s = [jnp.zeros((bp, SSD_STATE, SSD_INNER), F32),
                   jnp.swapaxes(state_ssd[j].reshape(bs, SSD_INNER, SSD_STATE), 1, 2)]
            c0s = [jnp.zeros((bp, 8, SSD_CONV_DIM), F32),
                   jnp.pad(state_ssd_conv[j], ((0, 0), (8 - (SSD_CONV_W - 1), 0), (0, 0)))]
            for grp, s0, c0, acc_s, acc_c in zip(groups, s0s, c0s, (new_ssd_p, new_ssd_s), (new_conv_p, new_conv_s)):
                ymix_ssd, st = _ssd_scan(z, xbc, dtr, cw, cb, dtb, an, dsk, nw, s0, c0, ymix_ssd, **grp)
                n, L, Lpad = grp['nseq'], grp['L'], grp['Lpad']
                acc_s.append(jnp.swapaxes(st, 1, 2).reshape(n, SSD_HEADS, SSD_HEAD_DIM, SSD_STATE))
                nconv = SSD_CONV_W - 1
                last = (grp['row_off'] + jnp.arange(n, dtype=I32)[:, None] * Lpad + (L - nconv)
                        + jnp.arange(nconv, dtype=I32)[None, :])
                acc_c.append(_take_rows(xbc, last.reshape(-1)).reshape(n, nconv, SSD_CONV_DIM))
            ymix = ymix_ssd
            wo = ssd_w_out[j].astype(BF16)
        else:
            w_in = gla_w_in[j].astype(BF16)
            nq = 2 * GLA_KEY_DIM + 2 * GLA_VAL_DIM
            qkvg, lr = _proj(h, [w_in[:, :nq], _pad_cols(w_in[:, nq:], LANES)], [F32, F32])
            w2 = jnp.pad(gla_w_gk2[j], ((0, LANES - GLA_GATE_RANK), (0, 0))).astype(BF16)
            b2 = gla_b_gk2[j][None]
            nw = gla_norm_w[j][None]
            s0s = [jnp.zeros((bp, GLA_VAL_DIM, GLA_DK), F32),
                   jnp.swapaxes(state_gla[j], 2, 3).reshape(bs, GLA_VAL_DIM, GLA_DK)]
            for grp, s0, acc, nsub in zip(groups, s0s, (new_gla_p, new_gla_s), (PROMPT_STEP_CHUNKS, 1)):
                ymix_gla, st = _gla_scan(qkvg, lr, w2, b2, nw, s0, ymix_gla, nsub=nsub, **grp)
                acc.append(jnp.swapaxes(st.reshape(grp['nseq'], GLA_HEADS, GLA_DV, GLA_DK), 2, 3))
            ymix = ymix_gla
            wo = gla_w_out[j].astype(BF16)

        wr = _pad_cols(jnp.concatenate([moe_w_grp[i], moe_w_exp[i]], axis=1), LANES).astype(BF16)
        br = _pad_cols(jnp.concatenate([moe_b_grp[i], moe_b_exp[i]])[None], LANES)
        h1, h1b, ri, rw, cnt = _post_mixer(ymix, wo, h, ln1_g[i][None], ln1_b[i][None], wr, br, alpha=alpha)
        ya, yb = _moe(h1b, ri, cnt, moe_w_gate, moe_w_up, moe_w_down, i)
        h = _combine(h1, ya, yb, rw, ln2_g[i][None], ln2_b[i][None], alpha=alpha)

    y_prompt = h[:off_s].reshape(bp, lp_pad, D_MODEL)[:, N_META:lp]
    y_sample = h[off_s:off_s + bs * lsm].reshape(bs, lsm, D_MODEL)
    return (y_prompt, y_sample, jnp.stack(new_ssd_p), jnp.stack(new_conv_p), jnp.stack(new_gla_p),
            jnp.stack(new_ssd_s), jnp.stack(new_conv_s), jnp.stack(new_gla_s))
```

```python
import functools

import jax
import jax.numpy as jnp
from jax import lax
from jax.experimental import pallas as pl
from jax.experimental.pallas import tpu as pltpu

F32 = jnp.float32
BF16 = jnp.bfloat16
I32 = jnp.int32
NT = (((1,), (1,)), ((), ()))
TN = (((0,), (0,)), ((), ()))
NN = (((1,), (0,)), ((), ()))

D_MODEL = 1024
N_META = 16
SSD_INNER = 2048
SSD_HEAD_DIM = 64
SSD_HEADS = 32
SSD_GROUPS = 4
SSD_STATE = 128
SSD_CONV_W = 4
SSD_CONV_DIM = 3072
GLA_HEADS = 4
GLA_KEY_DIM = 512
GLA_VAL_DIM = 1024
GLA_DK = 128
GLA_DV = 256
GLA_GATE_RANK = 16
GLA_GATE_NORM = 16.0
MOE_GROUPS = 4
MOE_EXP_PER_GROUP = 8
MOE_EXPERTS = 32
EXPERT_FF = 512
NORM_EPS = 1e-5

LANES = 128
VMEM_LIMIT_MB = 56
ROW_TILE = 512
POST_TILE = 1024
PROMPT_CHUNK = 128
PROMPT_STEP_CHUNKS = 2
GLA_SUB = 8
EXPERT_BLOCK = 256
EXPERT_STEP_BLOCKS = 2
INV_CHUNK = 1024


def _params(sem):
    return pltpu.CompilerParams(dimension_semantics=sem, vmem_limit_bytes=VMEM_LIMIT_MB << 20)


def _sigmoid(x):
    return 1.0 / (1.0 + jnp.exp(-x))


def _softplus(x):
    return jnp.maximum(x, 0.0) + jnp.log1p(jnp.exp(-jnp.abs(x)))


def _iota(shape, dim):
    return lax.broadcasted_iota(I32, shape, dim)


def _exact_dot(a_b, x, dims):
    hi = x.astype(BF16)
    r1 = x - hi.astype(F32)
    mid = r1.astype(BF16)
    lo = (r1 - mid.astype(F32)).astype(BF16)
    f = lambda t: lax.dot_general(a_b, t, dims, preferred_element_type=F32)
    return (f(hi) + f(mid)) + f(lo)


def _proj_kernel(x_ref, *refs):
    n = len(refs) // 2
    xb = x_ref[...].astype(BF16)
    for w_ref, o_ref in zip(refs[:n], refs[n:]):
        o_ref[...] = jnp.dot(xb, w_ref[...], preferred_element_type=F32).astype(o_ref.dtype)


def _proj(x, ws, out_dtypes):
    rows, k = x.shape
    grid = (rows // ROW_TILE,)
    in_specs = [pl.BlockSpec((ROW_TILE, k), lambda i: (i, 0))]
    in_specs += [pl.BlockSpec(w.shape, lambda i: (0, 0)) for w in ws]
    out_specs = [pl.BlockSpec((ROW_TILE, w.shape[1]), lambda i: (i, 0)) for w in ws]
    out_shape = [jax.ShapeDtypeStruct((rows, w.shape[1]), dt) for w, dt in zip(ws, out_dtypes)]
    return pl.pallas_call(
        _proj_kernel, grid=grid, in_specs=in_specs, out_specs=out_specs, out_shape=out_shape,
        compiler_params=_params(("parallel",)))(x, *ws)


def _ssd_kernel(z_ref, xbc_ref, dt_ref, cw_ref, cb_ref, dtb_ref, an_ref, dsk_ref, nw_ref, s0_ref, c0_ref,
                yprev_ref, y_ref, sout_ref, st_ref, xc_ref, u_ref, *, C, L):
    del yprev_ref
    c = pl.program_id(1)
    hist = 8

    @pl.when(c == 0)
    def _():
        st_ref[...] = s0_ref[0]
        xc_ref[0:hist, :] = c0_ref[0]

    @pl.when(c > 0)
    def _():
        xc_ref[0:hist, :] = xc_ref[C:C + hist, :]

    xc_ref[hist:hist + C, :] = xbc_ref[...].astype(F32)

    for j in range(0, SSD_CONV_DIM, 512):
        sl = slice(j, j + 512)
        acc = cb_ref[:, sl] + xc_ref[hist:hist + C, sl] * cw_ref[3:4, sl]
        for w in range(SSD_CONV_W - 1):
            off = hist - (SSD_CONV_W - 1) + w
            acc = acc + xc_ref[off:off + C, sl] * cw_ref[w:w + 1, sl]
        u_ref[:, sl] = acc * _sigmoid(acc)

    row = _iota((C, LANES), 0) + c * C
    dt = jnp.where(row < L, _softplus(dt_ref[...] + dtb_ref[...]), 0.0)
    loga = dt * an_ref[...]
    ti = _iota((C, C), 0)
    si = _iota((C, C), 1)
    tril = si <= ti
    eye_b = (_iota((LANES, LANES), 0) == _iota((LANES, LANES), 1)).astype(BF16)
    cum = _exact_dot(tril.astype(BF16), loga, NN)
    cum_t = _exact_dot(eye_b, cum, NT)
    dt_t = _exact_dot(eye_b, dt, NT)
    w_t = dt_t * jnp.exp(cum_t[:, C - 1:C] - cum_t)
    src_t = cum_t - jnp.log(dt_t)
    e_last = jnp.exp(cum[C - 1:C, :])
    lane = _iota((C, LANES), 1)
    lane_s = _iota((SSD_STATE, LANES), 1)
    lane_r = _iota((1, LANES), 1)
    hpg = SSD_HEADS // SSD_GROUPS

    for g in range(SSD_GROUPS):
        b_f = u_ref[:, SSD_INNER + g * SSD_STATE:SSD_INNER + (g + 1) * SSD_STATE]
        c_f = u_ref[:, SSD_INNER + (SSD_GROUPS + g) * SSD_STATE:SSD_INNER + (SSD_GROUPS + g + 1) * SSD_STATE]
        b_b = b_f.astype(BF16)
        cb = lax.dot_general(c_f.astype(BF16), b_b, NT, preferred_element_type=F32)
        b_t = lax.dot_general(eye_b, b_b, NT, preferred_element_type=F32)
        ypairs = []
        for jp in range(hpg // 2):
            p = g * (hpg // 2) + jp
            cs = slice(p * LANES, (p + 1) * LANES)
            xs_p = u_ref[:, cs]
            xs_b = xs_p.astype(BF16)
            st_p = st_ref[:, cs]
            st_b = st_p.astype(BF16)
            res, upd = [], []
            for h in (2 * p, 2 * p + 1):
                ccol = cum[:, h:h + 1]
                dec = jnp.where(tril, jnp.exp(ccol - src_t[h:h + 1, :]), 0.0)
                wmat = (cb * dec).astype(BF16)
                c_e = (c_f * jnp.exp(ccol)).astype(BF16)
                res.append(jnp.dot(wmat, xs_b, preferred_element_type=F32)
                           + jnp.dot(c_e, st_b, preferred_element_type=F32))
                b_w = (b_t * w_t[h:h + 1, :]).astype(BF16)
                upd.append(jnp.dot(b_w, xs_b, preferred_element_type=F32))
            y_p = jnp.where(lane < SSD_HEAD_DIM, res[0], res[1]) + dsk_ref[:, cs] * xs_p
            e_p = jnp.where(lane_r < SSD_HEAD_DIM, e_last[:, 2 * p:2 * p + 1], e_last[:, 2 * p + 1:2 * p + 2])
            st_ref[:, cs] = st_p * e_p + jnp.where(lane_s < SSD_HEAD_DIM, upd[0], upd[1])
            ypairs.append(y_p)
        gs = slice(g * 512, (g + 1) * 512)
        yg = jnp.concatenate(ypairs, axis=1)
        zg = z_ref[:, gs].astype(F32)
        yg = yg * (zg * _sigmoid(zg))
        ms = jnp.mean(yg * yg, axis=-1, keepdims=True)
        y_ref[:, gs] = (yg * lax.rsqrt(ms + NORM_EPS) * nw_ref[:, gs]).astype(y_ref.dtype)

    @pl.when(c == pl.num_programs(1) - 1)
    def _():
        sout_ref[0] = st_ref[...]


def _ssd_scan(z, xbc, dt, cw, cb, dtb, an, dsk, nw, s0_t, c0, y_prev, *, row_off, nseq, nwalk, L, Lpad, C):
    nc = Lpad // C
    base = row_off // C

    def rmap(b, c):
        return (base + b * nc + c, 0)

    const = lambda b, c: (0, 0)
    in_specs = [
        pl.BlockSpec((C, SSD_INNER), rmap),
        pl.BlockSpec((C, SSD_CONV_DIM), rmap),
        pl.BlockSpec((C, LANES), rmap),
        pl.BlockSpec((SSD_CONV_W, SSD_CONV_DIM), const),
        pl.BlockSpec((1, SSD_CONV_DIM), const),
        pl.BlockSpec((1, LANES), const),
        pl.BlockSpec((1, LANES), const),
        pl.BlockSpec((1, SSD_INNER), const),
        pl.BlockSpec((1, SSD_INNER), const),
        pl.BlockSpec((1, SSD_STATE, SSD_INNER), lambda b, c: (jnp.minimum(b, nseq - 1), 0, 0)),
        pl.BlockSpec((1, 8, SSD_CONV_DIM), lambda b, c: (jnp.minimum(b, nseq - 1), 0, 0)),
        pl.BlockSpec(memory_space=pl.ANY),
    ]
    out_specs = [
        pl.BlockSpec((C, SSD_INNER), rmap),
        pl.BlockSpec((1, SSD_STATE, SSD_INNER), lambda b, c: (b, 0, 0)),
    ]
    out_shape = [
        jax.ShapeDtypeStruct(y_prev.shape, y_prev.dtype),
        jax.ShapeDtypeStruct((nwalk, SSD_STATE, SSD_INNER), F32),
    ]
    scratch = [
        pltpu.VMEM((SSD_STATE, SSD_INNER), F32),
        pltpu.VMEM((C + 8, SSD_CONV_DIM), F32),
        pltpu.VMEM((C, SSD_CONV_DIM), F32),
    ]
    y, st = pl.pallas_call(
        functools.partial(_ssd_kernel, C=C, L=L), grid=(nwalk, nc), in_specs=in_specs, out_specs=out_specs,
        out_shape=out_shape, scratch_shapes=scratch, input_output_aliases={11: 0},
        compiler_params=_params(("arbitrary", "arbitrary")))(z, xbc, dt, cw, cb, dtb, an, dsk, nw, s0_t, c0, y_prev)
    return y, st[:nseq]


def _gla_kernel(q_ref, k_ref, v_ref, g_ref, lr_ref, w2_ref, b2_ref, nw_ref, s0_ref,
                oprev_ref, o_ref, sout_ref, st_ref, *, C, L, nsub):
    del oprev_ref
    c = pl.program_id(1)
    sub = GLA_SUB
    nb = C // sub

    @pl.when(c == 0)
    def _():
        st_ref[...] = s0_ref[0]

    ti = _iota((C, C), 0)
    si = _iota((C, C), 1)
    tril = si <= ti
    tril_b = tril.astype(BF16)
    diag_mask = jnp.logical_and(tril, (ti // sub) == (si // sub))
    lane_o = si % sub
    scale = GLA_DK ** -0.5

    intra = {}
    for s in range(nsub):
        rs = slice(s * C, (s + 1) * C)
        first_row = (c * nsub + s) * C
        gk = jnp.dot(lr_ref[rs, :].astype(BF16), w2_ref[...], preferred_element_type=F32) + b2_ref[...]
        logg = (jnp.minimum(gk, 0.0) - jnp.log1p(jnp.exp(-jnp.abs(gk)))) * (1.0 / GLA_GATE_NORM)
        logg = jnp.where(_iota((C, GLA_KEY_DIM), 0) + first_row < L, logg, 0.0)
        cum_all = _exact_dot(tril_b, logg, NN)
        valid_v = _iota((C, GLA_DV), 0) + first_row < L
        for h in range(GLA_HEADS):
            ks = slice(h * GLA_DK, (h + 1) * GLA_DK)
            vs = slice(h * GLA_DV, (h + 1) * GLA_DV)
            q = q_ref[rs, ks].astype(F32) * scale
            k = k_ref[rs, ks].astype(F32)
            v = jnp.where(valid_v, v_ref[rs, vs].astype(F32), 0.0)
            v_b = v.astype(BF16)
            cum = cum_all[:, ks]
            if nb > 1:
                q_parts, k_parts = [], []
                for i in range(1, nb):
                    r_i = cum[i * sub - 1:i * sub, :]
                    qi = q[i * sub:(i + 1) * sub, :] * jnp.exp(cum[i * sub:(i + 1) * sub, :] - r_i)
                    pieces = [jnp.zeros((i * sub, GLA_DK), F32), qi]
                    if C - (i + 1) * sub > 0:
                        pieces.append(jnp.zeros((C - (i + 1) * sub, GLA_DK), F32))
                    q_parts.append(jnp.concatenate(pieces, axis=0))
                    ki = k[0:i * sub, :] * jnp.exp(r_i - cum[0:i * sub, :])
                    k_parts.append(jnp.concatenate([ki, jnp.zeros((C - i * sub, GLA_DK), F32)], axis=0))
                q_st = jnp.concatenate(q_parts, axis=1).astype(BF16)
                k_st = jnp.concatenate(k_parts, axis=1).astype(BF16)
                a_off = lax.dot_general(q_st, k_st, NT, preferred_element_type=F32)
            else:
                a_off = jnp.zeros((C, C), F32)
            q3 = q.reshape(nb, sub, GLA_DK)
            k3 = k.reshape(nb, sub, GLA_DK)
            c3 = cum.reshape(nb, sub, GLA_DK)
            a_diag = jnp.zeros((C, C), F32)
            for o in range(sub):
                p_o = q3 * jnp.exp(jnp.minimum(c3 - c3[:, o:o + 1, :], 0.0)) * k3[:, o:o + 1, :]
                a_diag = jnp.where(lane_o == o, jnp.sum(p_o.reshape(C, GLA_DK), axis=-1, keepdims=True), a_diag)
            a = a_off + jnp.where(diag_mask, a_diag, 0.0)
            y_intra = jnp.dot(a.astype(BF16), v_b, preferred_element_type=F32)
            last = cum[C - 1:C, :]
            q_e = (q * jnp.exp(cum)).astype(BF16)
            k_e = (k * jnp.exp(last - cum)).astype(BF16)
            intra[s, h] = (y_intra, q_e, k_e, v_b, jnp.exp(last))

    for h in range(GLA_HEADS):
        vs = slice(h * GLA_DV, (h + 1) * GLA_DV)
        st = st_ref[vs, :]
        for s in range(nsub):
            rs = slice(s * C, (s + 1) * C)
            y_intra, q_e, k_e, v_b, e_last = intra[s, h]
            y = y_intra + lax.dot_general(q_e, st.astype(BF16), NT, preferred_element_type=F32)
            st = st * e_last + lax.dot_general(v_b, k_e, TN, preferred_element_type=F32)
            ms = jnp.mean(y * y, axis=-1, keepdims=True)
            gate = g_ref[rs, vs].astype(F32)
            o_ref[rs, vs] = (y * lax.rsqrt(ms + NORM_EPS) * nw_ref[...]
                             * (gate * _sigmoid(gate))).astype(o_ref.dtype)
        st_ref[vs, :] = st

    @pl.when(c == pl.num_programs(1) - 1)
    def _():
        sout_ref[0] = st_ref[...]


def _gla_scan(qkvg, lr, w2, b2, nw, s0_t, o_prev, *, row_off, nseq, nwalk, L, Lpad, C, nsub):
    blk = nsub * C
    nc = Lpad // blk
    base = row_off // blk
    const = lambda b, c: (0, 0)
    in_specs = [
        pl.BlockSpec((blk, GLA_KEY_DIM), lambda b, c: (base + b * nc + c, 0)),
        pl.BlockSpec((blk, GLA_KEY_DIM), lambda b, c: (base + b * nc + c, 1)),
        pl.BlockSpec((blk, GLA_VAL_DIM), lambda b, c: (base + b * nc + c, 1)),
        pl.BlockSpec((blk, GLA_VAL_DIM), lambda b, c: (base + b * nc + c, 2)),
        pl.BlockSpec((blk, LANES), lambda b, c: (base + b * nc + c, 0)),
        pl.BlockSpec((LANES, GLA_KEY_DIM), const),
        pl.BlockSpec((1, GLA_KEY_DIM), const),
        pl.BlockSpec((1, GLA_DV), const),
        pl.BlockSpec((1, GLA_VAL_DIM, GLA_DK), lambda b, c: (jnp.minimum(b, nseq - 1), 0, 0)),
        pl.BlockSpec(memory_space=pl.ANY),
    ]
    out_specs = [
        pl.BlockSpec((blk, GLA_VAL_DIM), lambda b, c: (base + b * nc + c, 0)),
        pl.BlockSpec((1, GLA_VAL_DIM, GLA_DK), lambda b, c: (b, 0, 0)),
    ]
    out_shape = [
        jax.ShapeDtypeStruct(o_prev.shape, o_prev.dtype),
        jax.ShapeDtypeStruct((nwalk, GLA_VAL_DIM, GLA_DK), F32),
    ]
    o, st = pl.pallas_call(
        functools.partial(_gla_kernel, C=C, L=L, nsub=nsub), grid=(nwalk, nc), in_specs=in_specs, out_specs=out_specs,
        out_shape=out_shape, scratch_shapes=[pltpu.VMEM((GLA_VAL_DIM, GLA_DK), F32)],
        input_output_aliases={9: 0},
        compiler_params=_params(("arbitrary", "arbitrary")))(qkvg, qkvg, qkvg, qkvg, lr, w2, b2, nw, s0_t, o_prev)
    return o, st[:nseq]


def _layer_norm(x, g, b):
    mu = jnp.mean(x, axis=-1, keepdims=True)
    xc = x - mu
    var = jnp.mean(xc * xc, axis=-1, keepdims=True)
    return xc * lax.rsqrt(var + NORM_EPS) * g + b


def _post_kernel(y_ref, wo_ref, h_ref, g_ref, b_ref, wr_ref, br_ref,
                 h1_ref, h1b_ref, ri_ref, rw_ref, cnt_ref, carry_ref, *, alpha):
    i = pl.program_id(0)
    tm = h_ref.shape[0]

    @pl.when(i == 0)
    def _():
        carry_ref[...] = jnp.zeros_like(carry_ref)

    mix = jnp.dot(y_ref[...], wo_ref[...], preferred_element_type=F32)
    h1 = _layer_norm(alpha * h_ref[...] + mix, g_ref[...], b_ref[...])
    h1_ref[...] = h1
    h_hi = h1.astype(BF16)
    h1b_ref[...] = h_hi
    logits = jnp.dot(h_hi, wr_ref[...], preferred_element_type=F32) + br_ref[...]
    lane = _iota((tm, LANES), 1)
    neg = -jnp.inf
    is_g = lane < MOE_GROUPS
    gl = jnp.where(is_g, logits, neg)
    gmax = jnp.max(gl, axis=-1, keepdims=True)
    gsel = jnp.min(jnp.where(gl == gmax, lane, LANES), axis=-1, keepdims=True)
    p_g = 1.0 / jnp.sum(jnp.where(is_g, jnp.exp(gl - gmax), 0.0), axis=-1, keepdims=True)
    eid = lane - MOE_GROUPS
    in_g = (eid >= 0) & (eid < MOE_EXPERTS) & ((eid // MOE_EXP_PER_GROUP) == gsel)
    el = jnp.where(in_g, logits, neg)
    v1 = jnp.max(el, axis=-1, keepdims=True)
    i1 = jnp.min(jnp.where(el == v1, lane, LANES), axis=-1, keepdims=True)
    el2 = jnp.where(lane == i1, neg, el)
    v2 = jnp.max(el2, axis=-1, keepdims=True)
    i2 = jnp.min(jnp.where(el2 == v2, lane, LANES), axis=-1, keepdims=True)
    t = jnp.exp(v2 - v1)
    w1 = p_g / (1.0 + t)
    w2 = p_g * t / (1.0 + t)
    e1 = i1 - MOE_GROUPS
    e2 = i2 - MOE_GROUPS
    oh = jnp.where(lane == e1, 1.0, 0.0) + jnp.where(lane == e2, 1.0, 0.0)
    strict = (_iota((tm, tm), 1) < _iota((tm, tm), 0)).astype(BF16)
    before = jnp.dot(strict, oh.astype(BF16), preferred_element_type=F32) + carry_ref[...]
    r1 = jnp.sum(jnp.where(lane == e1, before, 0.0), axis=-1, keepdims=True)
    r2 = jnp.sum(jnp.where(lane == e2, before, 0.0), axis=-1, keepdims=True)
    carry_ref[...] = carry_ref[...] + jnp.sum(oh, axis=0, keepdims=True)
    cnt_ref[...] = carry_ref[...]
    packed = jnp.where(lane == 0, e1.astype(F32), jnp.where(lane == 1, e2.astype(F32), jnp.where(
        lane == 2, r1, jnp.where(lane == 3, r2, 0.0))))
    ri_ref[...] = packed.T[0:8, :]
    rw_ref[...] = jnp.where(lane == 0, w1, jnp.where(lane == 1, w2, 0.0))


def _post_mixer(y, wo, h, g, b, wr, br, *, alpha):
    rows, kin = y.shape
    grid = (rows // POST_TILE,)
    rmap = lambda i: (i, 0)
    const = lambda i: (0, 0)
    in_specs = [
        pl.BlockSpec((POST_TILE, kin), rmap),
        pl.BlockSpec((kin, D_MODEL), const),
        pl.BlockSpec((POST_TILE, D_MODEL), rmap),
        pl.BlockSpec((1, D_MODEL), const),
        pl.BlockSpec((1, D_MODEL), const),
        pl.BlockSpec((D_MODEL, LANES), const),
        pl.BlockSpec((1, LANES), const),
    ]
    out_specs = [
        pl.BlockSpec((POST_TILE, D_MODEL), rmap),
        pl.BlockSpec((POST_TILE, D_MODEL), rmap),
        pl.BlockSpec((8, POST_TILE), lambda i: (0, i)),
        pl.BlockSpec((POST_TILE, LANES), rmap),
        pl.BlockSpec((1, LANES), const),
    ]
    out_shape = [
        jax.ShapeDtypeStruct((rows, D_MODEL), F32),
        jax.ShapeDtypeStruct((rows, D_MODEL), BF16),
        jax.ShapeDtypeStruct((8, rows), F32),
        jax.ShapeDtypeStruct((rows, LANES), F32),
        jax.ShapeDtypeStruct((1, LANES), F32),
    ]
    return pl.pallas_call(
        functools.partial(_post_kernel, alpha=alpha), grid=grid, in_specs=in_specs, out_specs=out_specs,
        out_shape=out_shape, scratch_shapes=[pltpu.VMEM((1, LANES), F32)],
        compiler_params=_params(("arbitrary",)))(y, wo, h, g, b, wr, br)


def _expert_kernel(be_ref, nu_ref, x_ref, *refs):
    nb = EXPERT_STEP_BLOCKS
    w_refs, o_ref, (wg_b, wu_b, wd_b) = refs[:3 * nb], refs[3 * nb], refs[3 * nb + 1:]
    i = pl.program_id(0)
    for half in range(nb):
        blk = i * nb + half
        wg_ref, wu_ref, wd_ref = w_refs[3 * half:3 * half + 3]
        rs = slice(half * EXPERT_BLOCK, (half + 1) * EXPERT_BLOCK)
        prev = be_ref[jnp.maximum(blk - 1, 0)]

        @pl.when(jnp.logical_or(blk == 0, be_ref[blk] != prev))
        def _():
            wg_b[...] = wg_ref[...].astype(BF16)
            wu_b[...] = wu_ref[...].astype(BF16)
            wd_b[...] = wd_ref[...].astype(BF16)

        @pl.when(blk < nu_ref[0])
        def _():
            x = x_ref[rs, :]
            hg = jnp.dot(x, wg_b[...], preferred_element_type=F32)
            hu = jnp.dot(x, wu_b[...], preferred_element_type=F32)
            hh = (hg * _sigmoid(hg) * hu).astype(BF16)
            o_ref[rs, :] = jnp.dot(hh, wd_b[...], preferred_element_type=F32).astype(o_ref.dtype)

        @pl.when(blk >= nu_ref[0])
        def _():
            o_ref[rs, :] = jnp.zeros((EXPERT_BLOCK, D_MODEL), o_ref.dtype)


def _experts(block_e, n_used, xb, w_gate, w_up, w_down, layer):
    nb = EXPERT_STEP_BLOCKS
    step_rows = nb * EXPERT_BLOCK
    nstep = xb.shape[0] // step_rows
    w_specs, w_args = [], []
    for half in range(nb):
        wmap = lambda i, be, nu, half=half: (layer, be[i * nb + half], 0, 0)
        w_specs += [pl.BlockSpec((None, None, D_MODEL, EXPERT_FF), wmap),
                    pl.BlockSpec((None, None, D_MODEL, EXPERT_FF), wmap),
                    pl.BlockSpec((None, None, EXPERT_FF, D_MODEL), wmap)]
        w_args += [w_gate, w_up, w_down]
    grid_spec = pltpu.PrefetchScalarGridSpec(
        num_scalar_prefetch=2, grid=(nstep,),
        in_specs=[pl.BlockSpec((step_rows, D_MODEL), lambda i, be, nu: (i, 0))] + w_specs,
        out_specs=pl.BlockSpec((step_rows, D_MODEL), lambda i, be, nu: (i, 0)),
        scratch_shapes=[pltpu.VMEM((D_MODEL, EXPERT_FF), BF16), pltpu.VMEM((D_MODEL, EXPERT_FF), BF16),
                        pltpu.VMEM((EXPERT_FF, D_MODEL), BF16)])
    return pl.pallas_call(
        _expert_kernel, grid_spec=grid_spec,
        out_shape=jax.ShapeDtypeStruct(xb.shape, BF16),
        compiler_params=_params(("arbitrary",)))(block_e, n_used, xb, *w_args)


def _combine_kernel(h_ref, ya_ref, yb_ref, rw_ref, g_ref, b_ref, o_ref, *, alpha):
    rw = rw_ref[...]
    ffn = ya_ref[...].astype(F32) * rw[:, 0:1] + yb_ref[...].astype(F32) * rw[:, 1:2]
    o_ref[...] = _layer_norm(alpha * h_ref[...] + ffn, g_ref[...], b_ref[...])


def _combine(h1, yab, rw, g, b, *, alpha):
    rows = h1.shape[0]
    tiles = rows // POST_TILE
    rmap = lambda i: (i, 0)
    const = lambda i: (0, 0)
    in_specs = [
        pl.BlockSpec((POST_TILE, D_MODEL), rmap), pl.BlockSpec((POST_TILE, D_MODEL), rmap),
        pl.BlockSpec((POST_TILE, D_MODEL), lambda i: (tiles + i, 0)), pl.BlockSpec((POST_TILE, LANES), rmap),
        pl.BlockSpec((1, D_MODEL), const), pl.BlockSpec((1, D_MODEL), const),
    ]
    return pl.pallas_call(
        functools.partial(_combine_kernel, alpha=alpha), grid=(rows // POST_TILE,), in_specs=in_specs,
        out_specs=pl.BlockSpec((POST_TILE, D_MODEL), rmap),
        out_shape=jax.ShapeDtypeStruct((rows, D_MODEL), F32),
        compiler_params=_params(("parallel",)))(h1, yab, yab, rw, g, b)


def _round_up(x, m):
    return (x + m - 1) // m * m


def _pad_cols(w, n):
    return jnp.pad(w, ((0, 0), (0, n - w.shape[1])))


def _take_rows(x, idx):
    return x.at[idx].get(mode='promise_in_bounds')


def _invert_kernel(npair_ref, dest_ref, init_ref, out_ref, inv_smem, buf0, buf1, sems, *, n_tok):
    nch = dest_ref.shape[0]
    bufs = (buf0, buf1)

    def chunk_copy(ch, slot):
        return pltpu.make_async_copy(dest_ref.at[ch], bufs[slot], sems.at[slot])

    init_copy = pltpu.make_async_copy(init_ref, inv_smem, sems.at[2])
    init_copy.start()
    chunk_copy(0, 0).start()
    init_copy.wait()

    def per_pair(pair, carry):
        for slot in range(2):
            ch = 2 * pair + slot
            chunk_copy(ch, slot).wait()

            @pl.when(ch + 1 < nch)
            def _():
                chunk_copy(ch + 1, 1 - slot).start()

            flat0 = ch * INV_CHUNK
            tok0 = jnp.where(flat0 >= n_tok, flat0 - n_tok, flat0)
            buf = bufs[slot]

            def per_slot(j, inner):
                inv_smem[buf[j]] = tok0 + j
                return inner

            lax.fori_loop(0, INV_CHUNK, per_slot, 0, unroll=16)
        return carry

    lax.fori_loop(0, npair_ref[0], per_pair, 0)
    out_copy = pltpu.make_async_copy(inv_smem, out_ref, sems.at[2])
    out_copy.start()
    out_copy.wait()


def _invert_slots(dest, init):
    n_tok = dest.shape[1]
    assert n_tok % INV_CHUNK == 0 and init.shape[0] % INV_CHUNK == 0
    any_spec = pl.BlockSpec(memory_space=pl.ANY)
    npair = jnp.full((1,), 2 * n_tok // INV_CHUNK // 2, I32)
    grid_spec = pltpu.PrefetchScalarGridSpec(
        num_scalar_prefetch=1, grid=(), in_specs=[any_spec, any_spec], out_specs=any_spec,
        scratch_shapes=[pltpu.SMEM(init.shape, I32), pltpu.SMEM((INV_CHUNK,), I32), pltpu.SMEM((INV_CHUNK,), I32),
                        pltpu.SemaphoreType.DMA((3,))])
    return pl.pallas_call(
        functools.partial(_invert_kernel, n_tok=n_tok), grid_spec=grid_spec,
        out_shape=jax.ShapeDtypeStruct(init.shape, I32),
    )(npair, dest.reshape(-1, INV_CHUNK), init)


def _moe(h1b, ri, cnt, w_gate, w_up, w_down, layer):
    rows = h1b.shape[0]
    nblk = _round_up((2 * rows + MOE_EXPERTS * (EXPERT_BLOCK - 1) + EXPERT_BLOCK - 1) // EXPERT_BLOCK,
                     EXPERT_STEP_BLOCKS)
    counts = cnt[0, :MOE_EXPERTS].astype(I32)
    pcounts = (counts + EXPERT_BLOCK - 1) // EXPERT_BLOCK * EXPERT_BLOCK
    pends = jnp.cumsum(pcounts)
    pstarts = pends - pcounts
    ri = ri.astype(I32)
    dest = _take_rows(pstarts, ri[0:2].reshape(-1)).reshape(2, rows) + ri[2:4]
    nslot = _round_up(nblk * EXPERT_BLOCK, INV_CHUNK)
    row_tok = _invert_slots(dest, jnp.arange(nslot, dtype=I32) % rows)[:nblk * EXPERT_BLOCK]
    blk_start = jnp.arange(nblk, dtype=I32) * EXPERT_BLOCK
    block_e = jnp.minimum(jnp.sum((pends[None, :] <= blk_start[:, None]).astype(I32), axis=1), MOE_EXPERTS - 1)
    n_used = (pends[-1:] // EXPERT_BLOCK).astype(I32)
    xb = _take_rows(h1b, row_tok)
    yb = _experts(block_e, n_used, xb, w_gate, w_up, w_down, layer)
    return _take_rows(yb, dest.reshape(-1))


def kernel(x_prompt, x_sample, state_ssd, state_ssd_conv, state_gla, meta_tokens, ssd_w_in, ssd_conv_w, ssd_conv_b, ssd_dt_bias, ssd_A_log, ssd_D, ssd_norm_w, ssd_w_out, gla_w_in, gla_w_gk2, gla_b_gk2, gla_norm_w, gla_w_out, ln1_g, ln1_b, moe_w_grp, moe_b_grp, moe_w_exp, moe_b_exp, moe_w_gate, moe_w_up, moe_w_down, ln2_g, ln2_b):
    bp, seq, _ = x_prompt.shape
    bs, lsm, _ = x_sample.shape
    depth = ln1_g.shape[0]
    alpha = (2.0 * depth) ** 0.25
    lp = N_META + seq
    cp = PROMPT_CHUNK
    lp_pad = _round_up(lp, PROMPT_STEP_CHUNKS * cp)
    cs = lsm
    off_s = bp * lp_pad
    assert off_s % cs == 0 and cs % GLA_SUB == 0 and cp % GLA_SUB == 0
    rows = _round_up(off_s + bs * lsm, max(POST_TILE, INV_CHUNK))

    pieces = []
    for b in range(bp):
        pieces += [meta_tokens.astype(F32), x_prompt[b], jnp.zeros((lp_pad - lp, D_MODEL), F32)]
    pieces += [x_sample.reshape(bs * lsm, D_MODEL), jnp.zeros((rows - off_s - bs * lsm, D_MODEL), F32)]
    h = jnp.concatenate(pieces, axis=0)

    groups = (dict(row_off=0, nseq=bp, nwalk=bp, L=lp, Lpad=lp_pad, C=cp),
              dict(row_off=off_s, nseq=bs, nwalk=(rows - off_s) // cs, L=lsm, Lpad=lsm, C=cs))
    ymix_ssd = jnp.zeros((rows, SSD_INNER), BF16)
    ymix_gla = jnp.zeros((rows, GLA_VAL_DIM), BF16)

    new_ssd_p, new_ssd_s, new_conv_p, new_conv_s, new_gla_p, new_gla_s = [], [], [], [], [], []
    for i in range(depth):
        j = i // 2
        if i % 2 == 0:
            w_in = ssd_w_in[j].astype(BF16)
            z, xbc, dtr = _proj(
                h, [w_in[:, :SSD_INNER], w_in[:, SSD_INNER:SSD_INNER + SSD_CONV_DIM],
                    _pad_cols(w_in[:, SSD_INNER + SSD_CONV_DIM:], LANES)], [F32, F32, F32])
            cw = ssd_conv_w[j]
            cb = ssd_conv_b[j][None]
            dtb = _pad_cols(ssd_dt_bias[j][None], LANES)
            an = _pad_cols(-jnp.exp(ssd_A_log[j].astype(F32))[None], LANES)
            dsk = jnp.repeat(ssd_D[j], SSD_HEAD_DIM)[None]
            nw = ssd_norm_w[j][None]
            s0s = [jnp.zeros((bp, SSD_STATE, SSD_INNER), F32),
                   jnp.swapaxes(state_ssd[j].reshape(bs, SSD_INNER, SSD_STATE), 1, 2)]
            c0s = [jnp.zeros((bp, 8, SSD_CONV_DIM), F32),
                   jnp.pad(state_ssd_conv[j], ((0, 0), (8 - (SSD_CONV_W - 1), 0), (0, 0)))]
            for grp, s0, c0, acc_s, acc_c in zip(groups, s0s, c0s, (new_ssd_p, new_ssd_s), (new_conv_p, new_conv_s)):
                ymix_ssd, st = _ssd_scan(z, xbc, dtr, cw, cb, dtb, an, dsk, nw, s0, c0, ymix_ssd, **grp)
                n, L, Lpad = grp['nseq'], grp['L'], grp['Lpad']
                acc_s.append(jnp.swapaxes(st, 1, 2).reshape(n, SSD_HEADS, SSD_HEAD_DIM, SSD_STATE))
                nconv = SSD_CONV_W - 1
                last = (grp['row_off'] + jnp.arange(n, dtype=I32)[:, None] * Lpad + (L - nconv)
                        + jnp.arange(nconv, dtype=I32)[None, :])
                acc_c.append(_take_rows(xbc, last.reshape(-1)).reshape(n, nconv, SSD_CONV_DIM))
            ymix = ymix_ssd
            wo = ssd_w_out[j].astype(BF16)
        else:
            w_in = gla_w_in[j].astype(BF16)
            nq = 2 * GLA_KEY_DIM + 2 * GLA_VAL_DIM
            qkvg, lr = _proj(h, [w_in[:, :nq], _pad_cols(w_in[:, nq:], LANES)], [F32, F32])
            w2 = jnp.pad(gla_w_gk2[j], ((0, LANES - GLA_GATE_RANK), (0, 0))).astype(BF16)
            b2 = gla_b_gk2[j][None]
            nw = gla_norm_w[j][None]
            s0s = [jnp.zeros((bp, GLA_VAL_DIM, GLA_DK), F32),
                   jnp.swapaxes(state_gla[j], 2, 3).reshape(bs, GLA_VAL_DIM, GLA_DK)]
            for grp, s0, acc, nsub in zip(groups, s0s, (new_gla_p, new_gla_s), (PROMPT_STEP_CHUNKS, 1)):
                ymix_gla, st = _gla_scan(qkvg, lr, w2, b2, nw, s0, ymix_gla, nsub=nsub, **grp)
                acc.append(jnp.swapaxes(st.reshape(grp['nseq'], GLA_HEADS, GLA_DV, GLA_DK), 2, 3))
            ymix = ymix_gla
            wo = gla_w_out[j].astype(BF16)

        wr = _pad_cols(jnp.concatenate([moe_w_grp[i], moe_w_exp[i]], axis=1), LANES).astype(BF16)
        br = _pad_cols(jnp.concatenate([moe_b_grp[i], moe_b_exp[i]])[None], LANES)
        h1, h1b, ri, rw, cnt = _post_mixer(ymix, wo, h, ln1_g[i][None], ln1_b[i][None], wr, br, alpha=alpha)
        yab = _moe(h1b, ri, cnt, moe_w_gate, moe_w_up, moe_w_down, i)
        h = _combine(h1, yab, rw, ln2_g[i][None], ln2_b[i][None], alpha=alpha)

    y_prompt = h[:off_s].reshape(bp, lp_pad, D_MODEL)[:, N_META:lp]
    y_sample = h[off_s:off_s + bs * lsm].reshape(bs, lsm, D_MODEL)
    return (y_prompt, y_sample, jnp.stack(new_ssd_p), jnp.stack(new_conv_p), jnp.stack(new_gla_p),
            jnp.stack(new_ssd_s), jnp.stack(new_conv_s), jnp.stack(new_gla_s))
```
